```python
import math
import jax, jax.numpy as jnp
from jax import lax
import numpy as np

D_MODEL = 1024
BATCH = 16
SEQ = 2048
DEPTH = 2

N_MEM = 256
EPS = 1e-6
ROPE_THETA = 10000.0
Q_BLOCK = 128

HY_CH = D_MODEL // 2
HY_ORDER = 2
HY_EMB = 33
HY_BANDS = (HY_EMB - 1) // 2
HY_FFN = 64
HY_FAST_PCT = 0.3
HY_SLOW_PCT = 1.5
HY_TARGET = 1e-2

MLA_HEADS = 8
MLA_NOPE = 64
MLA_ROPE = 32
MLA_V = 64
MLA_QK = MLA_NOPE + MLA_ROPE
MLA_Q_RANK = D_MODEL // 4
MLA_KV_RANK = D_MODEL // 8

IN_EVEN = (HY_ORDER + 1) * HY_CH + MLA_Q_RANK + MLA_KV_RANK + MLA_ROPE
MIX_EVEN = HY_CH + MLA_HEADS * MLA_V

DIFF_HEADS = 8
DIFF_HD = D_MODEL // DIFF_HEADS // 2
SUBLN_EPS = 1e-5

X_HEADS = 4
X_HD = D_MODEL // X_HEADS

D_FF = 4 * D_MODEL

N_EVEN = (DEPTH + 1) // 2
N_ODD = DEPTH // 2

kernel_name = "hybrid_hyena_mla_diffattn_encoder"


def rms_norm(x, g, eps=EPS):
    xf = x.astype(jnp.float32)
    y = xf * lax.rsqrt(jnp.mean(xf * xf, axis=-1, keepdims=True) + eps)
    return (y * g.astype(jnp.float32)).astype(x.dtype)


def rope_tables(seq_len, dim):
    inv = ROPE_THETA ** (-jnp.arange(0, dim, 2, dtype=jnp.float32) / dim)
    ang = jnp.arange(seq_len, dtype=jnp.float32)[:, None] * inv[None, :]
    return jnp.cos(ang), jnp.sin(ang)


def apply_rope(x, cos, sin):
    c = cos[None, :, None, :].astype(x.dtype)
    s = sin[None, :, None, :].astype(x.dtype)
    x1, x2 = jnp.split(x, 2, axis=-1)
    return jnp.concatenate([x1 * c - x2 * s, x2 * c + x1 * s], axis=-1)


def short_conv(u, w, b):
    up = jnp.pad(u, ((0, 0), (1, 1), (0, 0)))
    return up[:, :-2] * w[0] + up[:, 1:-1] * w[1] + up[:, 2:] * w[2] + b


def hyena_filters(seq_len, w1, b1, w2, b2, w3, freq):
    f32 = jnp.float32
    t = jnp.linspace(0.0, 1.0, seq_len, dtype=f32)[:, None]
    w = (2.0 * math.pi / seq_len) * jnp.arange(seq_len, dtype=f32)[:, None]
    bands = jnp.linspace(1e-4, HY_BANDS - 1, HY_BANDS, dtype=f32)[None, :]
    feats = jnp.concatenate([t, jnp.cos(bands * w), -jnp.sin(bands * w)], axis=-1)
    fr = freq.astype(f32)
    a = jnp.sin(fr * (feats @ w1.astype(f32) + b1.astype(f32)))
    a = jnp.sin(fr * (a @ w2.astype(f32) + b2.astype(f32)))
    h = (a @ w3.astype(f32)).reshape(seq_len, HY_ORDER, 2, HY_CH)
    max_decay = math.log(HY_TARGET) / HY_FAST_PCT
    min_decay = math.log(HY_TARGET) / HY_SLOW_PCT
    deltas = jnp.linspace(min_decay, max_decay, HY_CH, dtype=f32)
    window = jnp.exp(-t * jnp.abs(deltas)[None, :])
    return h * window[:, None, None, :]


def two_sided_long_conv(z, h_fwd, h_bwd):
    L = z.shape[1]
    k = jnp.concatenate([h_fwd, jnp.zeros((1, h_fwd.shape[1]), h_fwd.dtype), h_bwd[:0:-1]], axis=0)
    kf = jnp.fft.rfft(k, n=2 * L, axis=0)
    zf = jnp.fft.rfft(z.astype(jnp.float32), n=2 * L, axis=1)
    y = jnp.fft.irfft(zf * kf[None], n=2 * L, axis=1)[:, :L]
    return y.astype(z.dtype)


def block_attention(q, k, v, scale):
    B, S, H, dk = q.shape
    nb = S // Q_BLOCK
    qb = jnp.moveaxis(q.reshape(B, nb, Q_BLOCK, H, dk), 1, 0)

    def one(qblk):
        s = jnp.einsum('bqhd,bkhd->bhqk', qblk, k).astype(jnp.float32) * scale
        p = jax.nn.softmax(s, axis=-1).astype(v.dtype)
        return jnp.einsum('bhqk,bkhe->bqhe', p, v)

    o = lax.map(one, qb)
    return jnp.moveaxis(o, 0, 1).reshape(B, S, H, v.shape[-1])


def diff_block_attention(q, k, v, lam):
    B, S, H2, d = q.shape
    H = H2 // 2
    nb = S // Q_BLOCK
    qb = jnp.moveaxis(q.reshape(B, nb, Q_BLOCK, H2, d), 1, 0)
    scale = d ** -0.5

    def one(qblk):
        s = jnp.einsum('bqhd,bkhd->bhqk', qblk, k).astype(jnp.float32) * scale
        p = jax.nn.softmax(s, axis=-1).reshape(B, H, 2, Q_BLOCK, S)
        a = (p[:, :, 0] - lam * p[:, :, 1]).astype(v.dtype)
        return jnp.einsum('bhqk,bkhe->bqhe', a, v)

    o = lax.map(one, qb)
    return jnp.moveaxis(o, 0, 1).reshape(B, S, H, v.shape[-1])


def even_mixer(h, w_in, conv_w, conv_b, hy_w1, hy_b1, hy_w2, hy_b2, hy_w3, hy_freq, hy_skip,
               q_norm, w_uq, kv_norm, w_ukv, w_out, cos_r, sin_r):
    B, S, _ = h.shape
    proj = h @ w_in
    c0 = (HY_ORDER + 1) * HY_CH
    c1 = c0 + MLA_Q_RANK
    c2 = c1 + MLA_KV_RANK
    hy, cq, ckv, kr = jnp.split(proj, [c0, c1, c2], axis=-1)

    hy = short_conv(hy, conv_w, conv_b)
    v, x1, x2 = jnp.split(hy, 3, axis=-1)
    filt = hyena_filters(S, hy_w1, hy_b1, hy_w2, hy_b2, hy_w3, hy_freq)
    z = v
    for o, gate in enumerate((x1, x2)):
        z = gate * (two_sided_long_conv(z, filt[:, o, 0], filt[:, o, 1]) + hy_skip[o].astype(z.dtype) * z)

    q = (rms_norm(cq, q_norm) @ w_uq).reshape(B, S, MLA_HEADS, MLA_QK)
    q_nope, q_pe = jnp.split(q, [MLA_NOPE], axis=-1)
    q_pe = apply_rope(q_pe, cos_r, sin_r)
    kv = (rms_norm(ckv, kv_norm) @ w_ukv).reshape(B, S, MLA_HEADS, MLA_NOPE + MLA_V)
    k_nope, vv = jnp.split(kv, [MLA_NOPE], axis=-1)
    k_pe = apply_rope(kr[:, :, None, :], cos_r, sin_r)
    qf = jnp.concatenate([q_nope, q_pe], axis=-1)
    kf = jnp.concatenate([k_nope, jnp.broadcast_to(k_pe, (B, S, MLA_HEADS, MLA_ROPE))], axis=-1)
    o_mla = block_attention(qf, kf, vv, MLA_QK ** -0.5).reshape(B, S, MLA_HEADS * MLA_V)

    return jnp.concatenate([z, o_mla], axis=-1) @ w_out


def odd_mixer(h, w_qkv, lq1, lk1, lq2, lk2, subln, w_out, cos_d, sin_d, lam_init):
    B, S, _ = h.shape
    q, k, v = jnp.split(h @ w_qkv, 3, axis=-1)
    q = apply_rope(q.reshape(B, S, 2 * DIFF_HEADS, DIFF_HD), cos_d, sin_d)
    k = apply_rope(k.reshape(B, S, 2 * DIFF_HEADS, DIFF_HD), cos_d, sin_d)
    v = v.reshape(B, S, DIFF_HEADS, 2 * DIFF_HD)
    f32 = jnp.float32
    lam = (jnp.exp(jnp.sum(lq1.astype(f32) * lk1.astype(f32)))
           - jnp.exp(jnp.sum(lq2.astype(f32) * lk2.astype(f32))) + lam_init)
    o = diff_block_attention(q, k, v, lam)
    o = rms_norm(o, subln, SUBLN_EPS) * (1.0 - lam_init)
    return o.reshape(B, S, D_MODEL) @ w_out


def cross_attention(h, mem_n, wq, wkv, wo):
    B, S, _ = h.shape
    M = mem_n.shape[1]
    q = (h @ wq).reshape(B, S, X_HEADS, X_HD)
    k, v = jnp.split(mem_n @ wkv, 2, axis=-1)
    k = k.reshape(B, M, X_HEADS, X_HD)
    v = v.reshape(B, M, X_HEADS, X_HD)
    s = jnp.einsum('bqhd,bkhd->bhqk', q, k).astype(jnp.float32) * (X_HD ** -0.5)
    p = jax.nn.softmax(s, axis=-1).astype(v.dtype)
    o = jnp.einsum('bhqk,bkhd->bqhd', p, v).reshape(B, S, D_MODEL)
    return o @ wo


def sq_relu_mlp(h, w_up, w_down):
    return jnp.square(jax.nn.relu(h @ w_up)) @ w_down


def setup_inputs(seed: int = 0) -> dict:
    key = jax.random.key(seed)
    keys = iter(jax.random.split(key, 48))
    f32 = jnp.float32
    D = D_MODEL
    E, O, L = N_EVEN, N_ODD, DEPTH

    def normal(shape, std):
        return std * jax.random.normal(next(keys), shape, f32)

    def gain(shape):
        return 1.0 + normal(shape, 0.02)

    return {
        "x": normal((BATCH, SEQ, D), 1.0),
        "mem": normal((BATCH, N_MEM, D), 1.0),
        "ev_w_in": normal((E, D, IN_EVEN), D ** -0.5),
        "ev_conv_w": normal((E, 3, (HY_ORDER + 1) * HY_CH), 3 ** -0.5),
        "ev_conv_b": normal((E, (HY_ORDER + 1) * HY_CH), 0.02),
        "hy_w1": normal((E, HY_EMB, HY_FFN), HY_EMB ** -0.5),
        "hy_b1": normal((E, HY_FFN), 0.1),
        "hy_w2": normal((E, HY_FFN, HY_FFN), HY_FFN ** -0.5),
        "hy_b2": normal((E, HY_FFN), 0.1),
        "hy_w3": normal((E, HY_FFN, HY_ORDER * 2 * HY_CH), 0.05 * HY_FFN ** -0.5),
        "hy_freq": gain((E, HY_FFN)),
        "hy_skip": normal((E, HY_ORDER, HY_CH), 0.5),
        "mla_q_norm": gain((E, MLA_Q_RANK)),
        "mla_w_uq": normal((E, MLA_Q_RANK, MLA_HEADS * MLA_QK), MLA_Q_RANK ** -0.5),
        "mla_kv_norm": gain((E, MLA_KV_RANK)),
        "mla_w_ukv": normal((E, MLA_KV_RANK, MLA_HEADS * (MLA_NOPE + MLA_V)), MLA_KV_RANK ** -0.5),
        "ev_w_out": normal((E, MIX_EVEN, D), MIX_EVEN ** -0.5),
        "od_w_qkv": normal((O, D, 3 * D), D ** -0.5),
        "dif_lq1": normal((O, DIFF_HD), 0.1),
        "dif_lk1": normal((O, DIFF_HD), 0.1),
        "dif_lq2": normal((O, DIFF_HD), 0.1),
        "dif_lk2": normal((O, DIFF_HD), 0.1),
        "dif_subln": gain((O, 2 * DIFF_HD)),
        "od_w_out": normal((O, D, D), D ** -0.5),
        "norm_mix": gain((L, D)),
        "norm_cross": gain((L, D)),
        "norm_mlp": gain((L, D)),
        "xa_wq": normal((L, D, D), D ** -0.5),
        "xa_wkv": normal((L, D, 2 * D), D ** -0.5),
        "xa_wo": normal((L, D, D), D ** -0.5),
        "mlp_up": normal((L, D, D_FF), D ** -0.5),
        "mlp_down": normal((L, D_FF, D), D_FF ** -0.5),
        "mem_norm": gain((D,)),
        "final_norm": gain((D,)),
    }


def reference(x, mem, ev_w_in, ev_conv_w, ev_conv_b, hy_w1, hy_b1, hy_w2, hy_b2, hy_w3, hy_freq,
              hy_skip, mla_q_norm, mla_w_uq, mla_kv_norm, mla_w_ukv, ev_w_out, od_w_qkv, dif_lq1,
              dif_lk1, dif_lq2, dif_lk2, dif_subln, od_w_out, norm_mix, norm_cross, norm_mlp, xa_wq,
              xa_wkv, xa_wo, mlp_up, mlp_down, mem_norm, final_norm):
    S = x.shape[1]
    mem_n = rms_norm(mem, mem_norm)
    cos_r, sin_r = rope_tables(S, MLA_ROPE)
    cos_d, sin_d = rope_tables(S, DIFF_HD)
    for i in range(DEPTH):
        j = i // 2
        h = rms_norm(x, norm_mix[i])
        if i % 2 == 0:
            x = x + even_mixer(h, ev_w_in[j], ev_conv_w[j], ev_conv_b[j], hy_w1[j], hy_b1[j],
                               hy_w2[j], hy_b2[j], hy_w3[j], hy_freq[j], hy_skip[j],
                               mla_q_norm[j], mla_w_uq[j], mla_kv_norm[j], mla_w_ukv[j],
                               ev_w_out[j], cos_r, sin_r)
        else:
            lam_init = 0.8 - 0.6 * math.exp(-0.3 * i)
            x = x + odd_mixer(h, od_w_qkv[j], dif_lq1[j], dif_lk1[j], dif_lq2[j], dif_lk2[j],
                              dif_subln[j], od_w_out[j], cos_d, sin_d, lam_init)
        x = x + cross_attention(rms_norm(x, norm_cross[i]), mem_n, xa_wq[i], xa_wkv[i], xa_wo[i])
        x = x + sq_relu_mlp(rms_norm(x, norm_mlp[i]), mlp_up[i], mlp_down[i])
    return rms_norm(x, final_norm)
```

```python
import functools
import math

import jax
import jax.numpy as jnp
import numpy as np
from jax import lax
from jax.experimental import pallas as pl
from jax.experimental.pallas import tpu as pltpu

F32 = jnp.float32
BF16 = jnp.bfloat16

D_MODEL = 1024
EPS = 1e-6
ROPE_THETA = 10000.0
HY_CH = 512
HY_ORDER = 2
HY_EMB = 33
HY_BANDS = (HY_EMB - 1) // 2
HY_FFN = 64
HY_FAST_PCT = 0.3
HY_SLOW_PCT = 1.5
HY_TARGET = 1e-2
MLA_HEADS = 8
MLA_NOPE = 64
MLA_ROPE = 32
MLA_V = 64
MLA_QK = MLA_NOPE + MLA_ROPE
MLA_Q_RANK = D_MODEL // 4
MLA_KV_RANK = D_MODEL // 8
DIFF_HEADS = 8
DIFF_HD = D_MODEL // DIFF_HEADS // 2
SUBLN_EPS = 1e-5
X_HEADS = 4
X_HD = D_MODEL // X_HEADS
D_FF = 4 * D_MODEL

LANES = 128
V7X_VMEM_BYTES = 64 * 1024 * 1024
VMEM_LIMIT = V7X_VMEM_BYTES - 8 * 1024 * 1024

ROW_TILE = 512
ATTN_Q_TILE = 256
HY_CH_TILE = 256


def _const_spec(shape):
    nd = len(shape)
    return pl.BlockSpec(shape, lambda *_: (0,) * nd, pipeline_mode=pl.Buffered(1))


def _params(*sem):
    return pltpu.CompilerParams(dimension_semantics=sem, vmem_limit_bytes=VMEM_LIMIT)


def _rms(x, g, eps):
    return x * lax.rsqrt(jnp.mean(x * x, axis=-1, keepdims=True) + eps) * g


def _rope(x, c, s_lo, s_hi, half):
    return x * c + pltpu.roll(x, LANES - half, 1) * s_lo + pltpu.roll(x, half, 1) * s_hi


def _dot(a, b):
    return jnp.dot(a, b, preferred_element_type=F32)


def _dot_nt(a, b):
    return lax.dot_general(a, b, (((1,), (1,)), ((), ())), preferred_element_type=F32)


def _dot_f32(a, b):
    return jnp.dot(a, b, preferred_element_type=F32, precision=lax.Precision.HIGHEST)


def _filter_kernel(feats_ref, w1_ref, b1_ref, w2_ref, b2_ref, fr_ref, w3f_ref, w3b_ref, t_ref, dl_ref,
                   cw_ref, sw_ref, cm_ref, sm_ref, kre_ref, kim_ref, *, inv_scale):
    fr = fr_ref[...]
    a = jnp.sin(fr * (_dot_f32(feats_ref[...], w1_ref[...]) + b1_ref[...]))
    a = jnp.sin(fr * (_dot_f32(a, w2_ref[...]) + b2_ref[...]))
    window = jnp.exp(-t_ref[...] * jnp.abs(dl_ref[...]))
    h_f = _dot_f32(a, w3f_ref[...]) * window
    h_b = _dot_f32(a, w3b_ref[...]) * window
    row = lax.broadcasted_iota(jnp.int32, h_b.shape, 0)
    h_b = jnp.where(row == 0, 0.0, h_b)

    def spec(mat_ref, h):
        hi = h.astype(BF16)
        lo = (h - hi.astype(F32)).astype(BF16)
        m = mat_ref[...]
        return _dot(m, hi) + _dot(m, lo)

    hs = h_f + h_b
    hd = h_f - h_b
    cs = spec(cm_ref, hs)
    cd = spec(cm_ref, hd)
    ss = spec(sm_ref, hs)
    sd = spec(sm_ref, hd)
    cw = cw_ref[...]
    sw = sw_ref[...]
    kre_ref[...] = (cw * cs + sw * ss) * inv_scale
    kim_ref[...] = (sw * cd - cw * sd) * inv_scale


def _hyena_spectra(seq, hy_w1, hy_b1, hy_w2, hy_b2, hy_w3, hy_freq, cmat, smat):
    f32 = F32
    n = 2 * seq
    t = jnp.linspace(0.0, 1.0, seq, dtype=f32)[:, None]
    w = (2.0 * math.pi / seq) * jnp.arange(seq, dtype=f32)[:, None]
    bands = jnp.linspace(1e-4, HY_BANDS - 1, HY_BANDS, dtype=f32)[None, :]
    feats = jnp.concatenate([t, jnp.cos(bands * w), -jnp.sin(bands * w)], axis=-1)
    feats = jnp.pad(feats, ((0, 0), (0, LANES - HY_EMB)))
    pad_f = LANES - HY_FFN
    w1 = jnp.pad(hy_w1.astype(f32), ((0, LANES - HY_EMB), (0, pad_f)))
    b1 = jnp.pad(hy_b1.astype(f32), (0, pad_f))[None, :]
    w2 = jnp.pad(hy_w2.astype(f32), ((0, pad_f), (0, pad_f)))
    b2 = jnp.pad(hy_b2.astype(f32), (0, pad_f))[None, :]
    fr = jnp.pad(hy_freq.astype(f32), (0, pad_f))[None, :]
    w3 = jnp.pad(hy_w3.astype(f32), ((0, pad_f), (0, 0))).reshape(LANES, HY_ORDER, 2, HY_CH)
    w3f = w3[:, :, 0, :].reshape(LANES, HY_ORDER * HY_CH)
    w3b = w3[:, :, 1, :].reshape(LANES, HY_ORDER * HY_CH)
    max_decay = math.log(HY_TARGET) / HY_FAST_PCT
    min_decay = math.log(HY_TARGET) / HY_SLOW_PCT
    deltas = jnp.linspace(min_decay, max_decay, HY_CH, dtype=f32)[None, :]
    half_w = (math.pi / n) * (jnp.arange(seq, dtype=f32) + 0.5)
    cw = jnp.cos(half_w)[:, None]
    sw = jnp.sin(half_w)[:, None]

    tc = HY_CH_TILE
    ncb = HY_CH // tc
    small = lambda shape: pl.BlockSpec(shape, lambda o, c: (0,) * len(shape))
    out = pl.pallas_call(
        functools.partial(_filter_kernel, inv_scale=2.0 / n),
        grid=(HY_ORDER, ncb),
        in_specs=[
            small((seq, LANES)), small((LANES, LANES)), small((1, LANES)), small((LANES, LANES)),
            small((1, LANES)), small((1, LANES)),
            pl.BlockSpec((LANES, tc), lambda o, c: (0, o * ncb + c)),
            pl.BlockSpec((LANES, tc), lambda o, c: (0, o * ncb + c)),
            small((seq, 1)),
            pl.BlockSpec((1, tc), lambda o, c: (0, c)),
            small((seq, 1)), small((seq, 1)),
            _const_spec((seq, seq)), _const_spec((seq, seq)),
        ],
        out_specs=[pl.BlockSpec((None, seq, tc), lambda o, c: (o, 0, c))] * 2,
        out_shape=[jax.ShapeDtypeStruct((HY_ORDER, seq, HY_CH), f32)] * 2,
        compiler_params=_params("arbitrary", "arbitrary"),
        name="hyena_filter_spectra",
    )(feats, w1, b1, w2, b2, fr, w3f, w3b, t, deltas, cw, sw, cmat, smat)
    return out


def _dft_matrices(seq):
    n4 = 8 * seq
    f = 2 * jnp.arange(seq, dtype=jnp.int32) + 1
    m = (f[:, None] * f[None, :]) % n4
    ang = m.astype(F32) * (2.0 * math.pi / n4)
    return jnp.cos(ang).astype(BF16), jnp.sin(ang).astype(BF16)


def _hyena_kernel(v_ref, x1_ref, x2_ref, wv_ref, wx1_ref, wx2_ref, bv_ref, bx1_ref, bx2_ref,
                  kre_ref, kim_ref, skip_ref, cm_ref, sm_ref, o_ref):
    seq = v_ref.shape[0]
    row = lax.broadcasted_iota(jnp.int32, v_ref.shape, 0)
    first = row == 0
    last = row == seq - 1

    def short_conv(u_ref, w_ref, b_ref):
        u = u_ref[...].astype(F32)
        w = w_ref[...]
        prev = jnp.where(first, 0.0, pltpu.roll(u, 1, 0))
        nxt = jnp.where(last, 0.0, pltpu.roll(u, seq - 1, 0))
        return prev * w[0:1] + u * w[1:2] + nxt * w[2:3] + b_ref[...]

    cm = cm_ref[...]
    sm = sm_ref[...]
    z = short_conv(v_ref, wv_ref, bv_ref)
    gate_refs = ((x1_ref, wx1_ref, bx1_ref), (x2_ref, wx2_ref, bx2_ref))
    skip = skip_ref[...]
    for o in range(HY_ORDER):
        zb = z.astype(BF16)
        az = _dot(cm, zb)
        bz = _dot(sm, zb)
        kre = kre_ref[o]
        kim = kim_ref[o]
        yre = (az * kre + bz * kim).astype(BF16)
        yim = (bz * kre - az * kim).astype(BF16)
        y = _dot(cm, yre) + _dot(sm, yim)
        z = short_conv(*gate_refs[o]) * (y + skip[o:o + 1] * z)
    o_ref[...] = z.astype(o_ref.dtype)


def _hyena(hy, conv_w, conv_b, kre, kim, skip, cmat, smat, batch, seq):
    tc = HY_CH_TILE
    ncb = HY_CH // tc
    hy3 = hy.reshape(batch, seq, 3 * HY_CH)
    cb = conv_b[None, :]
    u_spec = lambda g: pl.BlockSpec((None, seq, tc), lambda c, b: (b, 0, g * ncb + c))
    w_spec = lambda g: pl.BlockSpec((3, tc), lambda c, b: (0, g * ncb + c))
    b_spec = lambda g: pl.BlockSpec((1, tc), lambda c, b: (0, g * ncb + c))
    k_spec = pl.BlockSpec((HY_ORDER, seq, tc), lambda c, b: (0, 0, c), pipeline_mode=pl.Buffered(1))
    return pl.pallas_call(
        _hyena_kernel,
        grid=(ncb, batch),
        in_specs=[u_spec(0), u_spec(1), u_spec(2), w_spec(0), w_spec(1), w_spec(2),
                  b_spec(0), b_spec(1), b_spec(2), k_spec, k_spec,
                  pl.BlockSpec((HY_ORDER, tc), lambda c, b: (0, c)),
                  _const_spec((seq, seq)), _const_spec((seq, seq))],
        out_specs=pl.BlockSpec((None, seq, tc), lambda c, b: (b, 0, c)),
        out_shape=jax.ShapeDtypeStruct((batch, seq, HY_CH), BF16),
        compiler_params=_params("arbitrary", "arbitrary"),
        name="hyena_long_conv",
    )(hy3, hy3, hy3, conv_w, conv_w, conv_w, cb, cb, cb, kre, kim, skip, cmat, smat)


def _even_in_kernel(x_ref, g_ref, win_ref, qg_ref, wuq_ref, kvg_ref, wuk_ref, wuv_ref,
                    c_ref, slo_ref, shi_ref, hy_ref, q_ref, k_ref, v_ref):
    xn = _rms(x_ref[...], g_ref[...], EPS).astype(BF16)
    proj = _dot(xn, win_ref[...])
    c0 = 3 * HY_CH
    c1 = c0 + MLA_Q_RANK
    c2 = c1 + MLA_KV_RANK
    hy_ref[...] = proj[:, :c0].astype(hy_ref.dtype)
    c = c_ref[...]
    s_lo = slo_ref[...]
    s_hi = shi_ref[...]
    half = MLA_ROPE // 2
    scale = MLA_QK ** -0.5

    qn = _rms(proj[:, c0:c1], qg_ref[...], EPS).astype(BF16)
    q = _dot(qn, wuq_ref[...])
    kvn = _rms(proj[:, c1:c2], kvg_ref[...], EPS).astype(BF16)
    kn = _dot(kvn, wuk_ref[...])
    v_ref[...] = _dot(kvn, wuv_ref[...]).astype(BF16)
    k_pe = _rope(pltpu.roll(proj[:, c2:c2 + LANES], MLA_NOPE, 1), c, s_lo, s_hi, half)
    for h in range(MLA_HEADS):
        blk = slice(h * LANES, (h + 1) * LANES)
        q_ref[:, blk] = (_rope(q[:, blk], c, s_lo, s_hi, half) * scale).astype(BF16)
        k_ref[:, blk] = (kn[:, blk] + k_pe).astype(BF16)


def _mla_rope_tables(seq):
    inv = ROPE_THETA ** (-jnp.arange(0, MLA_ROPE, 2, dtype=F32) / MLA_ROPE)
    ang = jnp.arange(seq, dtype=F32)[:, None] * inv[None, :]
    cos, sin = jnp.cos(ang), jnp.sin(ang)
    half = MLA_ROPE // 2
    one = jnp.ones((seq, MLA_NOPE), F32)
    zero = jnp.zeros((seq, MLA_NOPE), F32)
    tail1 = jnp.ones((seq, LANES - MLA_QK), F32)
    tail0 = jnp.zeros((seq, LANES - MLA_QK), F32)
    zh = jnp.zeros((seq, half), F32)
    c = jnp.concatenate([one, cos, cos, tail1], axis=1)
    s_lo = jnp.concatenate([zero, -sin, zh, tail0], axis=1)
    s_hi = jnp.concatenate([zero, zh, sin, tail0], axis=1)
    return c, s_lo, s_hi


def _pad_heads(w, heads, width):
    rows = w.shape[0]
    w = w.reshape(rows, heads, width)
    return jnp.pad(w, ((0, 0), (0, 0), (0, LANES - width))).reshape(rows, heads * LANES)


def _even_in(x2d, g, w_in, q_norm, w_uq, kv_norm, w_ukv, seq):
    rows = x2d.shape[0]
    tm = ROW_TILE
    n_in = w_in.shape[1]
    n_pad = -(-(n_in + LANES - MLA_ROPE) // LANES) * LANES
    win = jnp.pad(w_in, ((0, 0), (0, n_pad - n_in))).astype(BF16)
    wuq = _pad_heads(w_uq, MLA_HEADS, MLA_QK).astype(BF16)
    wkv = w_ukv.reshape(MLA_KV_RANK, MLA_HEADS, MLA_NOPE + MLA_V)
    wuk = _pad_heads(wkv[:, :, :MLA_NOPE].reshape(MLA_KV_RANK, -1), MLA_HEADS, MLA_NOPE).astype(BF16)
    wuv = _pad_heads(wkv[:, :, MLA_NOPE:].reshape(MLA_KV_RANK, -1), MLA_HEADS, MLA_V).astype(BF16)
    c, s_lo, s_hi = _mla_rope_tables(seq)
    tiles_per_seq = seq // tm
    row_spec = lambda width: pl.BlockSpec((tm, width), lambda i: (i, 0))
    tab_spec = pl.BlockSpec((tm, LANES), lambda i: (i % tiles_per_seq, 0))
    hw = MLA_HEADS * LANES
    return pl.pallas_call(
        _even_in_kernel,
        grid=(rows // tm,),
        in_specs=[row_spec(D_MODEL), _const_spec((1, D_MODEL)), _const_spec(win.shape),
                  _const_spec((1, MLA_Q_RANK)), _const_spec(wuq.shape),
                  _const_spec((1, MLA_KV_RANK)), _const_spec(wuk.shape), _const_spec(wuv.shape),
                  tab_spec, tab_spec, tab_spec],
        out_specs=[row_spec(3 * HY_CH), row_spec(hw), row_spec(hw), row_spec(hw)],
        out_shape=[jax.ShapeDtypeStruct((rows, 3 * HY_CH), BF16)] + [jax.ShapeDtypeStruct((rows, hw), BF16)] * 3,
        compiler_params=_params("arbitrary"),
        name="even_in_proj",
    )(x2d, g[None, :], win, q_norm[None, :], wuq, kv_norm[None, :], wuk, wuv, c, s_lo, s_hi)


def _attn_kernel(q_ref, k_ref, v_ref, o_ref, *, heads):
    for h in range(heads):
        blk = slice(h * LANES, (h + 1) * LANES)
        s = _dot_nt(q_ref[:, blk], k_ref[:, blk])
        e = jnp.exp(s - jnp.max(s, axis=-1, keepdims=True))
        l = jnp.sum(e, axis=-1, keepdims=True)
        o = _dot(e.astype(BF16), v_ref[:, blk])
        o_ref[:, blk] = (o / l).astype(o_ref.dtype)


def _attention(q, k, v, batch, seq, heads):
    tq = ATTN_Q_TILE
    hw = heads * LANES
    q3, k3, v3 = (a.reshape(batch, seq, hw) for a in (q, k, v))
    q_spec = pl.BlockSpec((None, tq, hw), lambda b, i: (b, i, 0))
    kv_spec = pl.BlockSpec((None, seq, hw), lambda b, i: (b, 0, 0))
    out = pl.pallas_call(
        functools.partial(_attn_kernel, heads=heads),
        grid=(batch, seq // tq),
        in_specs=[q_spec, kv_spec, kv_spec],
        out_specs=q_spec,
        out_shape=jax.ShapeDtypeStruct((batch, seq, hw), BF16),
        compiler_params=_params("arbitrary", "arbitrary"),
        name="mla_attention",
    )(q3, k3, v3)
    return out.reshape(batch * seq, hw)


def _proj_residual_kernel(*refs):
    x_ref, o_ref = refs[0], refs[-1]
    acc = x_ref[...]
    for a_ref, w_ref in zip(refs[1:-1:2], refs[2:-1:2]):
        acc = acc + _dot(a_ref[...], w_ref[...])
    o_ref[...] = acc


def _proj_residual(x2d, pairs):
    rows = x2d.shape[0]
    tm = ROW_TILE
    row_spec = lambda width: pl.BlockSpec((tm, width), lambda i: (i, 0))
    in_specs = [row_spec(D_MODEL)]
    args = [x2d]
    for a, w in pairs:
        in_specs += [row_spec(a.shape[1]), _const_spec(w.shape)]
        args += [a, w]
    return pl.pallas_call(
        _proj_residual_kernel,
        grid=(rows // tm,),
        in_specs=in_specs,
        out_specs=row_spec(D_MODEL),
        out_shape=jax.ShapeDtypeStruct((rows, D_MODEL), F32),
        compiler_params=_params("arbitrary"),
        name="mixer_out_proj",
    )(*args)


def _odd_in_kernel(x_ref, g_ref, w_ref, c_ref, slo_ref, shi_ref, q_ref, k_ref, v_ref):
    xn = _rms(x_ref[...], g_ref[...], EPS).astype(BF16)
    qkv = _dot(xn, w_ref[...])
    c = c_ref[...]
    s_lo = slo_ref[...]
    s_hi = shi_ref[...]
    half = DIFF_HD // 2
    scale = DIFF_HD ** -0.5
    for h in range(DIFF_HEADS):
        blk = slice(h * LANES, (h + 1) * LANES)
        kblk = slice(D_MODEL + h * LANES, D_MODEL + (h + 1) * LANES)
        q_ref[:, blk] = (_rope(qkv[:, blk], c, s_lo, s_hi, half) * scale).astype(BF16)
        k_ref[:, blk] = _rope(qkv[:, kblk], c, s_lo, s_hi, half).astype(BF16)
    v_ref[...] = qkv[:, 2 * D_MODEL:].astype(BF16)


def _diff_rope_tables(seq):
    inv = ROPE_THETA ** (-jnp.arange(0, DIFF_HD, 2, dtype=F32) / DIFF_HD)
    ang = jnp.arange(seq, dtype=F32)[:, None] * inv[None, :]
    cos, sin = jnp.cos(ang), jnp.sin(ang)
    zero = jnp.zeros_like(sin)
    reps = LANES // DIFF_HD
    c = jnp.tile(jnp.concatenate([cos, cos], axis=1), (1, reps))
    s_lo = jnp.tile(jnp.concatenate([-sin, zero], axis=1), (1, reps))
    s_hi = jnp.tile(jnp.concatenate([zero, sin], axis=1), (1, reps))
    return c, s_lo, s_hi


def _odd_in(x2d, g, w_qkv, seq):
    rows = x2d.shape[0]
    tm = ROW_TILE
    c, s_lo, s_hi = _diff_rope_tables(seq)
    tiles_per_seq = seq // tm
    row_spec = lambda width: pl.BlockSpec((tm, width), lambda i: (i, 0))
    tab_spec = pl.BlockSpec((tm, LANES), lambda i: (i % tiles_per_seq, 0))
    w = w_qkv.astype(BF16)
    return pl.pallas_call(
        _odd_in_kernel,
        grid=(rows // tm,),
        in_specs=[row_spec(D_MODEL), _const_spec((1, D_MODEL)), _const_spec(w.shape),
                  tab_spec, tab_spec, tab_spec],
        out_specs=[row_spec(D_MODEL)] * 3,
        out_shape=[jax.ShapeDtypeStruct((rows, D_MODEL), BF16)] * 3,
        compiler_params=_params("arbitrary"),
        name="odd_in_proj",
    )(x2d, g[None, :], w, c, s_lo, s_hi)


def _diff_attn_kernel(lam_ref, q_ref, k_ref, v_ref, g_ref, o_ref, *, lam_init):
    lp = lam_ref[...]
    lam = (jnp.exp(jnp.sum(lp[0:1] * lp[1:2], axis=-1, keepdims=True))
           - jnp.exp(jnp.sum(lp[2:3] * lp[3:4], axis=-1, keepdims=True)) + lam_init)
    g = g_ref[...] * (1.0 - lam_init)
    tq = q_ref.shape[0]
    first_head = lax.broadcasted_iota(jnp.int32, (tq, LANES), 1) < DIFF_HD
    for h in range(DIFF_HEADS):
        blk = slice(h * LANES, (h + 1) * LANES)
        q = q_ref[:, blk]
        k = k_ref[:, blk]
        zero = jnp.zeros_like(q)
        s1 = _dot_nt(jnp.where(first_head, q, zero), k)
        s2 = _dot_nt(jnp.where(first_head, zero, q), k)
        e1 = jnp.exp(s1 - jnp.max(s1, axis=-1, keepdims=True))
        e2 = jnp.exp(s2 - jnp.max(s2, axis=-1, keepdims=True))
        l1 = jnp.sum(e1, axis=-1, keepdims=True)
        l2 = jnp.sum(e2, axis=-1, keepdims=True)
        a = (e1 - (lam * l1 / l2) * e2).astype(BF16)
        o = _dot(a, v_ref[:, blk]) / l1
        o_ref[:, blk] = (o * lax.rsqrt(jnp.mean(o * o, axis=-1, keepdims=True) + SUBLN_EPS) * g).astype(o_ref.dtype)


def _diff_attention(q, k, v, lam_params, subln, lam_init, batch, seq):
    tq = ATTN_Q_TILE
    q3, k3, v3 = (a.reshape(batch, seq, D_MODEL) for a in (q, k, v))
    q_spec = pl.BlockSpec((None, tq, D_MODEL), lambda b, i: (b, i, 0))
    kv_spec = pl.BlockSpec((None, seq, D_MODEL), lambda b, i: (b, 0, 0))
    lam_tile = jnp.pad(jnp.stack(lam_params).astype(F32), ((0, 4), (0, LANES - DIFF_HD)))
    out = pl.pallas_call(
        functools.partial(_diff_attn_kernel, lam_init=lam_init),
        grid=(batch, seq // tq),
        in_specs=[pl.BlockSpec((8, LANES), lambda b, i: (0, 0)), q_spec, kv_spec, kv_spec,
                  pl.BlockSpec((1, LANES), lambda b, i: (0, 0))],
        out_specs=q_spec,
        out_shape=jax.ShapeDtypeStruct((batch, seq, D_MODEL), BF16),
        compiler_params=_params("arbitrary", "arbitrary"),
        name="diff_attention",
    )(lam_tile, q3, k3, v3, subln[None, :])
    return out.reshape(batch * seq, D_MODEL)


def _mem_kv_kernel(mem_ref, g_ref, wkv_ref, kt_ref, v_ref):
    mn = _rms(mem_ref[...], g_ref[...], EPS).astype(BF16)
    for layer in range(wkv_ref.shape[0]):
        kv = _dot(mn, wkv_ref[layer])
        kt_ref[layer] = kv[:, :D_MODEL].T.astype(BF16)
        v_ref[layer] = kv[:, D_MODEL:].astype(BF16)


def _mem_kv(mem, mem_norm, xa_wkv):
    batch, n_mem, _ = mem.shape
    depth = xa_wkv.shape[0]
    w = xa_wkv.astype(BF16)
    return pl.pallas_call(
        _mem_kv_kernel,
        grid=(batch,),
        in_specs=[pl.BlockSpec((None, n_mem, D_MODEL), lambda b: (b, 0, 0)), _const_spec((1, D_MODEL)),
                  _const_spec(w.shape)],
        out_specs=[pl.BlockSpec((depth, None, D_MODEL, n_mem), lambda b: (0, b, 0, 0)),
                   pl.BlockSpec((depth, None, n_mem, D_MODEL), lambda b: (0, b, 0, 0))],
        out_shape=[jax.ShapeDtypeStruct((depth, batch, D_MODEL, n_mem), BF16),
                   jax.ShapeDtypeStruct((depth, batch, n_mem, D_MODEL), BF16)],
        compiler_params=_params("arbitrary"),
        name="memory_kv",
    )(mem, mem_norm[None, :], w)


def _cross_mlp_kernel(x_ref, gc_ref, wq_ref, kt_ref, v_ref, wo_ref, gm_ref, wup_ref, wdn_ref, gf_ref, o_ref,
                      *, final_norm):
    x = x_ref[...]
    hn = _rms(x, gc_ref[...], EPS).astype(BF16)
    q = (_dot(hn, wq_ref[...]) * (X_HD ** -0.5)).astype(BF16)
    heads = []
    for h in range(X_HEADS):
        blk = slice(h * X_HD, (h + 1) * X_HD)
        s = _dot(q[:, blk], kt_ref[blk, :])
        e = jnp.exp(s - jnp.max(s, axis=-1, keepdims=True))
        l = jnp.sum(e, axis=-1, keepdims=True)
        heads.append((_dot(e.astype(BF16), v_ref[:, blk]) / l).astype(BF16))
    x = x + _dot(jnp.concatenate(heads, axis=1), wo_ref[...])
    hn = _rms(x, gm_ref[...], EPS).astype(BF16)
    u = jnp.maximum(_dot(hn, wup_ref[...]), 0.0)
    x = x + _dot((u * u).astype(BF16), wdn_ref[...])
    if final_norm:
        x = _rms(x, gf_ref[...], EPS)
    o_ref[...] = x


def _cross_mlp(x2d, g_cross, wq, kt, v, wo, g_mlp, w_up, w_down, g_final, seq, final_norm):
    rows = x2d.shape[0]
    tm = ROW_TILE
    tiles_per_seq = seq // tm
    n_mem = v.shape[1]
    row_spec = pl.BlockSpec((tm, D_MODEL), lambda i: (i, 0))
    vec_spec = _const_spec((1, D_MODEL))
    return pl.pallas_call(
        functools.partial(_cross_mlp_kernel, final_norm=final_norm),
        grid=(rows // tm,),
        in_specs=[row_spec, vec_spec, _const_spec((D_MODEL, D_MODEL)),
                  pl.BlockSpec((None, D_MODEL, n_mem), lambda i: (i // tiles_per_seq, 0, 0)),
                  pl.BlockSpec((None, n_mem, D_MODEL), lambda i: (i // tiles_per_seq, 0, 0)),
                  _const_spec((D_MODEL, D_MODEL)), vec_spec, _const_spec((D_MODEL, D_FF)),
                  _const_spec((D_FF, D_MODEL)), vec_spec],
        out_specs=row_spec,
        out_shape=jax.ShapeDtypeStruct((rows, D_MODEL), F32),
        compiler_params=_params("arbitrary"),
        name="cross_attn_mlp",
    )(x2d, g_cross[None, :], wq.astype(BF16), kt, v, wo.astype(BF16), g_mlp[None, :],
      w_up.astype(BF16), w_down.astype(BF16), g_final[None, :])


def kernel(x, mem, ev_w_in, ev_conv_w, ev_conv_b, hy_w1, hy_b1, hy_w2, hy_b2, hy_w3, hy_freq, hy_skip,
           mla_q_norm, mla_w_uq, mla_kv_norm, mla_w_ukv, ev_w_out, od_w_qkv, dif_lq1, dif_lk1, dif_lq2,
           dif_lk2, dif_subln, od_w_out, norm_mix, norm_cross, norm_mlp, xa_wq, xa_wkv, xa_wo, mlp_up,
           mlp_down, mem_norm, final_norm):
    batch, seq, d = x.shape
    depth = norm_mix.shape[0]
    assert d == D_MODEL and seq % ROW_TILE == 0 and seq % ATTN_Q_TILE == 0
    x2d = x.reshape(batch * seq, d)
    kt_all, v_all = _mem_kv(mem, mem_norm, xa_wkv)
    cmat, smat = _dft_matrices(seq)
    for i in range(depth):
        j = i // 2
        if i % 2 == 0:
            kre, kim = _hyena_spectra(seq, hy_w1[j], hy_b1[j], hy_w2[j], hy_b2[j], hy_w3[j], hy_freq[j],
                                      cmat, smat)
            hy, q, k, v = _even_in(x2d, norm_mix[i], ev_w_in[j], mla_q_norm[j], mla_w_uq[j],
                                   mla_kv_norm[j], mla_w_ukv[j], seq)
            z = _hyena(hy, ev_conv_w[j], ev_conv_b[j], kre, kim, hy_skip[j], cmat, smat, batch, seq)
            o = _attention(q, k, v, batch, seq, MLA_HEADS)
            w_out = ev_w_out[j]
            w_mla = w_out[HY_CH:].reshape(MLA_HEADS, MLA_V, d)
            w_mla = jnp.pad(w_mla, ((0, 0), (0, LANES - MLA_V), (0, 0))).reshape(MLA_HEADS * LANES, d)
            x2d = _proj_residual(x2d, [(z.reshape(batch * seq, HY_CH), w_out[:HY_CH].astype(BF16)),
                                       (o, w_mla.astype(BF16))])
        else:
            lam_init = 0.8 - 0.6 * math.exp(-0.3 * i)
            q, k, v = _odd_in(x2d, norm_mix[i], od_w_qkv[j], seq)
            o = _diff_attention(q, k, v, (dif_lq1[j], dif_lk1[j], dif_lq2[j], dif_lk2[j]), dif_subln[j],
                                lam_init, batch, seq)
            x2d = _proj_residual(x2d, [(o, od_w_out[j].astype(BF16))])
        x2d = _cross_mlp(x2d, norm_cross[i], xa_wq[i], kt_all[i], v_all[i], xa_wo[i], norm_mlp[i],
                         mlp_up[i], mlp_down[i], final_norm, seq, final_norm=(i == depth - 1))
    return x2d.reshape(batch, seq, d)
```

```python
import functools
import math

import jax
import jax.numpy as jnp
import numpy as np
from jax import lax
from jax.experimental import pallas as pl
from jax.experimental.pallas import tpu as pltpu

F32 = jnp.float32
BF16 = jnp.bfloat16

D_MODEL = 1024
EPS = 1e-6
ROPE_THETA = 10000.0
HY_CH = 512
HY_ORDER = 2
HY_EMB = 33
HY_BANDS = (HY_EMB - 1) // 2
HY_FFN = 64
HY_FAST_PCT = 0.3
HY_SLOW_PCT = 1.5
HY_TARGET = 1e-2
MLA_HEADS = 8
MLA_NOPE = 64
MLA_ROPE = 32
MLA_V = 64
MLA_QK = MLA_NOPE + MLA_ROPE
MLA_Q_RANK = D_MODEL // 4
MLA_KV_RANK = D_MODEL // 8
DIFF_HEADS = 8
DIFF_HD = D_MODEL // DIFF_HEADS // 2
SUBLN_EPS = 1e-5
X_HEADS = 4
X_HD = D_MODEL // X_HEADS
D_FF = 4 * D_MODEL

LOG2E = math.log2(math.e)
LANES = 128
V7X_VMEM_BYTES = 64 * 1024 * 1024
VMEM_LIMIT = V7X_VMEM_BYTES - 8 * 1024 * 1024

ROW_TILE = 512
ATTN_Q_TILE = 256
SOFTMAX_ROWS = 16
SOFTMAX_KEYS = 512
HY_CH_TILE = 256
HY_ROW_CHUNK = 512


def _const_spec(shape):
    nd = len(shape)
    return pl.BlockSpec(shape, lambda *_: (0,) * nd, pipeline_mode=pl.Buffered(1))


def _params(*sem):
    return pltpu.CompilerParams(dimension_semantics=sem, vmem_limit_bytes=VMEM_LIMIT)


def _rms(x, g, eps):
    return x * lax.rsqrt(jnp.mean(x * x, axis=-1, keepdims=True) + eps) * g


def _rope(x, c, s_lo, s_hi, half):
    return x * c + pltpu.roll(x, LANES - half, 1) * s_lo + pltpu.roll(x, half, 1) * s_hi


def _dot(a, b):
    return jnp.dot(a, b, preferred_element_type=F32)


def _dot_nt(a, b):
    return lax.dot_general(a, b, (((1,), (1,)), ((), ())), preferred_element_type=F32)


def _dot_f32(a, b):
    return jnp.dot(a, b, preferred_element_type=F32, precision=lax.Precision.HIGHEST)


def _filter_kernel(feats_ref, w1_ref, b1_ref, w2_ref, b2_ref, fr_ref, w3f_ref, w3b_ref, t_ref, dl_ref,
                   cw_ref, sw_ref, cm_ref, sm_ref, kre_ref, kim_ref, act_ref, *, inv_scale):
    @pl.when((pl.program_id(0) == 0) & (pl.program_id(1) == 0))
    def _():
        fr = fr_ref[...]
        a = jnp.sin(fr * (_dot_f32(feats_ref[...], w1_ref[...]) + b1_ref[...]))
        act_ref[...] = jnp.sin(fr * (_dot_f32(a, w2_ref[...]) + b2_ref[...]))

    a = act_ref[...]
    window = jnp.exp(-t_ref[...] * jnp.abs(dl_ref[...]))
    h_f = _dot_f32(a, w3f_ref[...]) * window
    h_b = _dot_f32(a, w3b_ref[...]) * window
    row = lax.broadcasted_iota(jnp.int32, h_b.shape, 0)
    h_b = jnp.where(row == 0, 0.0, h_b)
    h = jnp.concatenate([h_f + h_b, h_f - h_b], axis=1)
    hi = h.astype(BF16)
    lo = (h - hi.astype(F32)).astype(BF16)
    tc = h_f.shape[1]
    for r in range(0, h.shape[0], HY_ROW_CHUNK):
        rows = slice(r, r + HY_ROW_CHUNK)
        c = _dot(cm_ref[rows, :], hi) + _dot(cm_ref[rows, :], lo)
        s = _dot(sm_ref[rows, :], hi) + _dot(sm_ref[rows, :], lo)
        cw = cw_ref[rows, :]
        sw = sw_ref[rows, :]
        kre_ref[rows, :] = (cw * c[:, :tc] + sw * s[:, :tc]) * inv_scale
        kim_ref[rows, :] = (sw * c[:, tc:] - cw * s[:, tc:]) * inv_scale


def _hyena_spectra(seq, hy_w1, hy_b1, hy_w2, hy_b2, hy_w3, hy_freq, cmat, smat):
    f32 = F32
    n = 2 * seq
    t = jnp.linspace(0.0, 1.0, seq, dtype=f32)[:, None]
    w = (2.0 * math.pi / seq) * jnp.arange(seq, dtype=f32)[:, None]
    bands = jnp.linspace(1e-4, HY_BANDS - 1, HY_BANDS, dtype=f32)[None, :]
    feats = jnp.concatenate([t, jnp.cos(bands * w), -jnp.sin(bands * w)], axis=-1)
    feats = jnp.pad(feats, ((0, 0), (0, LANES - HY_EMB)))
    pad_f = LANES - HY_FFN
    w1 = jnp.pad(hy_w1.astype(f32), ((0, LANES - HY_EMB), (0, pad_f)))
    b1 = jnp.pad(hy_b1.astype(f32), (0, pad_f))[None, :]
    w2 = jnp.pad(hy_w2.astype(f32), ((0, pad_f), (0, pad_f)))
    b2 = jnp.pad(hy_b2.astype(f32), (0, pad_f))[None, :]
    fr = jnp.pad(hy_freq.astype(f32), (0, pad_f))[None, :]
    w3 = jnp.pad(hy_w3.astype(f32), ((0, pad_f), (0, 0))).reshape(LANES, HY_ORDER, 2, HY_CH)
    w3f = w3[:, :, 0, :].reshape(LANES, HY_ORDER * HY_CH)
    w3b = w3[:, :, 1, :].reshape(LANES, HY_ORDER * HY_CH)
    max_decay = math.log(HY_TARGET) / HY_FAST_PCT
    min_decay = math.log(HY_TARGET) / HY_SLOW_PCT
    deltas = jnp.linspace(min_decay, max_decay, HY_CH, dtype=f32)[None, :]
    half_w = (math.pi / n) * (jnp.arange(seq, dtype=f32) + 0.5)
    cw = jnp.cos(half_w)[:, None]
    sw = jnp.sin(half_w)[:, None]

    tc = HY_CH_TILE
    ncb = HY_CH // tc
    small = lambda shape: pl.BlockSpec(shape, lambda o, c: (0,) * len(shape))
    out = pl.pallas_call(
        functools.partial(_filter_kernel, inv_scale=2.0 / n),
        grid=(HY_ORDER, ncb),
        in_specs=[
            small((seq, LANES)), small((LANES, LANES)), small((1, LANES)), small((LANES, LANES)),
            small((1, LANES)), small((1, LANES)),
            pl.BlockSpec((LANES, tc), lambda o, c: (0, o * ncb + c)),
            pl.BlockSpec((LANES, tc), lambda o, c: (0, o * ncb + c)),
            small((seq, 1)),
            pl.BlockSpec((1, tc), lambda o, c: (0, c)),
            small((seq, 1)), small((seq, 1)),
            _const_spec((seq, seq)), _const_spec((seq, seq)),
        ],
        out_specs=[pl.BlockSpec((None, seq, tc), lambda o, c: (o, 0, c))] * 2,
        out_shape=[jax.ShapeDtypeStruct((HY_ORDER, seq, HY_CH), f32)] * 2,
        scratch_shapes=[pltpu.VMEM((seq, LANES), f32)],
        compiler_params=_params("arbitrary", "arbitrary"),
        name="hyena_filter_spectra",
    )(feats, w1, b1, w2, b2, fr, w3f, w3b, t, deltas, cw, sw, cmat, smat)
    return out


def _dft_matrices(seq):
    n4 = 8 * seq
    f = 2 * jnp.arange(seq, dtype=jnp.int32) + 1
    m = (f[:, None] * f[None, :]) % n4
    ang = m.astype(F32) * (2.0 * math.pi / n4)
    return jnp.cos(ang).astype(BF16), jnp.sin(ang).astype(BF16)


def _hyena_kernel(v_ref, x1_ref, x2_ref, wv_ref, wx1_ref, wx2_ref, bv_ref, bx1_ref, bx2_ref,
                  kre_ref, kim_ref, skip_ref, cm_ref, sm_ref, o_ref, z_ref, yre_ref, yim_ref):
    seq = v_ref.shape[0]
    row = lax.broadcasted_iota(jnp.int32, v_ref.shape, 0)
    first = row == 0
    last = row == seq - 1

    def short_conv(u_ref, w_ref, b_ref):
        u = u_ref[...].astype(F32)
        w = w_ref[...]
        prev = jnp.where(first, 0.0, pltpu.roll(u, 1, 0))
        nxt = jnp.where(last, 0.0, pltpu.roll(u, seq - 1, 0))
        return prev * w[0:1] + u * w[1:2] + nxt * w[2:3] + b_ref[...]

    chunks = [slice(r, r + HY_ROW_CHUNK) for r in range(0, seq, HY_ROW_CHUNK)]
    z_ref[...] = short_conv(v_ref, wv_ref, bv_ref)
    gate_refs = ((x1_ref, wx1_ref, bx1_ref), (x2_ref, wx2_ref, bx2_ref))
    skip = skip_ref[...]
    for o in range(HY_ORDER):
        zb = z_ref[...].astype(BF16)
        for rows in chunks:
            az = _dot(cm_ref[rows, :], zb)
            bz = _dot(sm_ref[rows, :], zb)
            kre = kre_ref[o, rows, :]
            kim = kim_ref[o, rows, :]
            yre_ref[rows, :] = (az * kre + bz * kim).astype(BF16)
            yim_ref[rows, :] = (bz * kre - az * kim).astype(BF16)
        gate = short_conv(*gate_refs[o])
        yre = yre_ref[...]
        yim = yim_ref[...]
        for rows in chunks:
            y = _dot(cm_ref[rows, :], yre) + _dot(sm_ref[rows, :], yim)
            z_new = gate[rows] * (y + skip[o:o + 1] * z_ref[rows, :])
            if o == HY_ORDER - 1:
                o_ref[rows, :] = z_new.astype(o_ref.dtype)
            else:
                z_ref[rows, :] = z_new


def _hyena(hy, conv_w, conv_b, kre, kim, skip, cmat, smat, batch, seq):
    tc = HY_CH_TILE
    ncb = HY_CH // tc
    hy3 = hy.reshape(batch, seq, 3 * HY_CH)
    cb = conv_b[None, :]
    u_spec = lambda g: pl.BlockSpec((None, seq, tc), lambda c, b: (b, 0, g * ncb + c))
    w_spec = lambda g: pl.BlockSpec((3, tc), lambda c, b: (0, g * ncb + c))
    b_spec = lambda g: pl.BlockSpec((1, tc), lambda c, b: (0, g * ncb + c))
    k_spec = pl.BlockSpec((HY_ORDER, seq, tc), lambda c, b: (0, 0, c), pipeline_mode=pl.Buffered(1))
    return pl.pallas_call(
        _hyena_kernel,
        grid=(ncb, batch),
        in_specs=[u_spec(0), u_spec(1), u_spec(2), w_spec(0), w_spec(1), w_spec(2),
                  b_spec(0), b_spec(1), b_spec(2), k_spec, k_spec,
                  pl.BlockSpec((HY_ORDER, tc), lambda c, b: (0, c)),
                  _const_spec((seq, seq)), _const_spec((seq, seq))],
        out_specs=pl.BlockSpec((None, seq, tc), lambda c, b: (b, 0, c)),
        out_shape=jax.ShapeDtypeStruct((batch, seq, HY_CH), BF16),
        scratch_shapes=[pltpu.VMEM((seq, tc), F32), pltpu.VMEM((seq, tc), BF16), pltpu.VMEM((seq, tc), BF16)],
        compiler_params=_params("arbitrary", "arbitrary"),
        name="hyena_long_conv",
    )(hy3, hy3, hy3, conv_w, conv_w, conv_w, cb, cb, cb, kre, kim, skip, cmat, smat)


def _even_in_kernel(x_ref, g_ref, win_ref, qg_ref, wuq_ref, kvg_ref, wuk_ref, wuv_ref, ones_ref,
                    c_ref, slo_ref, shi_ref, hy_ref, q_ref, k_ref, v_ref):
    xn = _rms(x_ref[...], g_ref[...], EPS).astype(BF16)
    proj = _dot(xn, win_ref[...])
    c0 = 3 * HY_CH
    c1 = c0 + MLA_Q_RANK
    c2 = c1 + MLA_KV_RANK
    hy_ref[...] = proj[:, :c0].astype(hy_ref.dtype)
    c = c_ref[...]
    s_lo = slo_ref[...]
    s_hi = shi_ref[...]
    half = MLA_ROPE // 2
    scale = MLA_QK ** -0.5 * LOG2E

    qn = _rms(proj[:, c0:c1], qg_ref[...], EPS).astype(BF16)
    q = _dot(qn, wuq_ref[...])
    kvn = _rms(proj[:, c1:c2], kvg_ref[...], EPS).astype(BF16)
    kn = _dot(kvn, wuk_ref[...])
    v_ref[...] = (_dot(kvn, wuv_ref[...]) + ones_ref[...]).astype(BF16)
    k_pe = _rope(pltpu.roll(proj[:, c2:c2 + LANES], MLA_NOPE, 1), c, s_lo, s_hi, half)
    for h in range(MLA_HEADS):
        blk = slice(h * LANES, (h + 1) * LANES)
        q_ref[:, blk] = (_rope(q[:, blk], c, s_lo, s_hi, half) * scale).astype(BF16)
        k_ref[:, blk] = (kn[:, blk] + k_pe).astype(BF16)


def _mla_rope_tables(seq):
    inv = ROPE_THETA ** (-jnp.arange(0, MLA_ROPE, 2, dtype=F32) / MLA_ROPE)
    ang = jnp.arange(seq, dtype=F32)[:, None] * inv[None, :]
    cos, sin = jnp.cos(ang), jnp.sin(ang)
    half = MLA_ROPE // 2
    one = jnp.ones((seq, MLA_NOPE), F32)
    zero = jnp.zeros((seq, MLA_NOPE), F32)
    tail1 = jnp.ones((seq, LANES - MLA_QK), F32)
    tail0 = jnp.zeros((seq, LANES - MLA_QK), F32)
    zh = jnp.zeros((seq, half), F32)
    c = jnp.concatenate([one, cos, cos, tail1], axis=1)
    s_lo = jnp.concatenate([zero, -sin, zh, tail0], axis=1)
    s_hi = jnp.concatenate([zero, zh, sin, tail0], axis=1)
    return c, s_lo, s_hi


def _pad_heads(w, heads, width):
    rows = w.shape[0]
    w = w.reshape(rows, heads, width)
    return jnp.pad(w, ((0, 0), (0, 0), (0, LANES - width))).reshape(rows, heads * LANES)


def _even_in(x2d, g, w_in, q_norm, w_uq, kv_norm, w_ukv, seq):
    rows = x2d.shape[0]
    tm = ROW_TILE
    n_in = w_in.shape[1]
    n_pad = -(-(n_in + LANES - MLA_ROPE) // LANES) * LANES
    win = jnp.pad(w_in, ((0, 0), (0, n_pad - n_in))).astype(BF16)
    wuq = _pad_heads(w_uq, MLA_HEADS, MLA_QK).astype(BF16)
    wkv = w_ukv.reshape(MLA_KV_RANK, MLA_HEADS, MLA_NOPE + MLA_V)
    wuk = _pad_heads(wkv[:, :, :MLA_NOPE].reshape(MLA_KV_RANK, -1), MLA_HEADS, MLA_NOPE).astype(BF16)
    wuv = _pad_heads(wkv[:, :, MLA_NOPE:].reshape(MLA_KV_RANK, -1), MLA_HEADS, MLA_V).astype(BF16)
    c, s_lo, s_hi = _mla_rope_tables(seq)
    tiles_per_seq = seq // tm
    row_spec = lambda width: pl.BlockSpec((tm, width), lambda i: (i, 0))
    tab_spec = pl.BlockSpec((tm, LANES), lambda i: (i % tiles_per_seq, 0))
    hw = MLA_HEADS * LANES
    ones_col = jnp.tile((jnp.arange(LANES) == MLA_V).astype(F32), MLA_HEADS)[None, :]
    return pl.pallas_call(
        _even_in_kernel,
        grid=(rows // tm,),
        in_specs=[row_spec(D_MODEL), _const_spec((1, D_MODEL)), _const_spec(win.shape),
                  _const_spec((1, MLA_Q_RANK)), _const_spec(wuq.shape),
                  _const_spec((1, MLA_KV_RANK)), _const_spec(wuk.shape), _const_spec(wuv.shape),
                  _const_spec((1, hw)), tab_spec, tab_spec, tab_spec],
        out_specs=[row_spec(3 * HY_CH), row_spec(hw), row_spec(hw), row_spec(hw)],
        out_shape=[jax.ShapeDtypeStruct((rows, 3 * HY_CH), BF16)] + [jax.ShapeDtypeStruct((rows, hw), BF16)] * 3,
        compiler_params=_params("arbitrary"),
        name="even_in_proj",
    )(x2d, g[None, :], win, q_norm[None, :], wuq, kv_norm[None, :], wuk, wuv, ones_col, c, s_lo, s_hi)


def _attn_kernel(q_ref, k_ref, v_ref, o_ref, *, heads, sum_lane):
    for h in range(heads):
        blk = slice(h * LANES, (h + 1) * LANES)
        s = _dot_nt(q_ref[:, blk], k_ref[:, blk])
        e = jnp.exp2(s - jnp.max(s, axis=-1, keepdims=True)).astype(BF16)
        o = _dot(e, v_ref[:, blk])
        l = o[:, sum_lane:sum_lane + 1]
        o_ref[:, blk] = (o / l).astype(o_ref.dtype)


def _attention(q, k, v, batch, seq, heads, sum_lane):
    tq = ATTN_Q_TILE
    hw = heads * LANES
    q3, k3, v3 = (a.reshape(batch, seq, hw) for a in (q, k, v))
    q_spec = pl.BlockSpec((None, tq, hw), lambda b, i: (b, i, 0))
    kv_spec = pl.BlockSpec((None, seq, hw), lambda b, i: (b, 0, 0))
    out = pl.pallas_call(
        functools.partial(_attn_kernel, heads=heads, sum_lane=sum_lane),
        grid=(batch, seq // tq),
        in_specs=[q_spec, kv_spec, kv_spec],
        out_specs=q_spec,
        out_shape=jax.ShapeDtypeStruct((batch, seq, hw), BF16),
        compiler_params=_params("arbitrary", "arbitrary"),
        name="mla_attention",
    )(q3, k3, v3)
    return out.reshape(batch * seq, hw)


def _proj_residual_kernel(*refs):
    x_ref, o_ref = refs[0], refs[-1]
    acc = x_ref[...]
    for a_ref, w_ref in zip(refs[1:-1:2], refs[2:-1:2]):
        acc = acc + _dot(a_ref[...], w_ref[...])
    o_ref[...] = acc


def _proj_residual(x2d, pairs):
    rows = x2d.shape[0]
    tm = ROW_TILE
    row_spec = lambda width: pl.BlockSpec((tm, width), lambda i: (i, 0))
    in_specs = [row_spec(D_MODEL)]
    args = [x2d]
    for a, w in pairs:
        in_specs += [row_spec(a.shape[1]), _const_spec(w.shape)]
        args += [a, w]
    return pl.pallas_call(
        _proj_residual_kernel,
        grid=(rows // tm,),
        in_specs=in_specs,
        out_specs=row_spec(D_MODEL),
        out_shape=jax.ShapeDtypeStruct((rows, D_MODEL), F32),
        compiler_params=_params("arbitrary"),
        name="mixer_out_proj",
    )(*args)


def _odd_in_kernel(x_ref, g_ref, w_ref, c_ref, slo_ref, shi_ref, q_ref, k_ref, v_ref):
    xn = _rms(x_ref[...], g_ref[...], EPS).astype(BF16)
    qkv = _dot(xn, w_ref[...])
    c = c_ref[...]
    s_lo = slo_ref[...]
    s_hi = shi_ref[...]
    half = DIFF_HD // 2
    scale = DIFF_HD ** -0.5 * LOG2E
    for h in range(DIFF_HEADS):
        blk = slice(h * LANES, (h + 1) * LANES)
        kblk = slice(D_MODEL + h * LANES, D_MODEL + (h + 1) * LANES)
        q_ref[:, blk] = (_rope(qkv[:, blk], c, s_lo, s_hi, half) * scale).astype(BF16)
        k_ref[:, blk] = _rope(qkv[:, kblk], c, s_lo, s_hi, half).astype(BF16)
    v_ref[...] = qkv[:, 2 * D_MODEL:].astype(BF16)


def _diff_rope_tables(seq):
    inv = ROPE_THETA ** (-jnp.arange(0, DIFF_HD, 2, dtype=F32) / DIFF_HD)
    ang = jnp.arange(seq, dtype=F32)[:, None] * inv[None, :]
    cos, sin = jnp.cos(ang), jnp.sin(ang)
    zero = jnp.zeros_like(sin)
    reps = LANES // DIFF_HD
    c = jnp.tile(jnp.concatenate([cos, cos], axis=1), (1, reps))
    s_lo = jnp.tile(jnp.concatenate([-sin, zero], axis=1), (1, reps))
    s_hi = jnp.tile(jnp.concatenate([zero, sin], axis=1), (1, reps))
    return c, s_lo, s_hi


def _odd_in(x2d, g, w_qkv, seq):
    rows = x2d.shape[0]
    tm = ROW_TILE
    c, s_lo, s_hi = _diff_rope_tables(seq)
    tiles_per_seq = seq // tm
    row_spec = lambda width: pl.BlockSpec((tm, width), lambda i: (i, 0))
    tab_spec = pl.BlockSpec((tm, LANES), lambda i: (i % tiles_per_seq, 0))
    w = w_qkv.astype(BF16)
    return pl.pallas_call(
        _odd_in_kernel,
        grid=(rows // tm,),
        in_specs=[row_spec(D_MODEL), _const_spec((1, D_MODEL)), _const_spec(w.shape),
                  tab_spec, tab_spec, tab_spec],
        out_specs=[row_spec(D_MODEL)] * 3,
        out_shape=[jax.ShapeDtypeStruct((rows, D_MODEL), BF16)] * 3,
        compiler_params=_params("arbitrary"),
        name="odd_in_proj",
    )(x2d, g[None, :], w, c, s_lo, s_hi)


def _diff_attn_kernel(lam_ref, q_ref, k_ref, v_ref, g_ref, o_ref, s_ref, e_ref, *, lam_init):
    lp = lam_ref[...]
    lam = (jnp.exp(jnp.sum(lp[0:1] * lp[1:2], axis=-1, keepdims=True))
           - jnp.exp(jnp.sum(lp[2:3] * lp[3:4], axis=-1, keepdims=True)) + lam_init)
    g = g_ref[...] * (1.0 - lam_init)
    tq = q_ref.shape[0]
    first_head = lax.broadcasted_iota(jnp.int32, (tq, LANES), 1) < DIFF_HD
    blocks = [slice(r, r + SOFTMAX_ROWS) for r in range(0, tq, SOFTMAX_ROWS)]

    def scores(h):
        blk = slice(h * LANES, (h + 1) * LANES)
        q = q_ref[:, blk]
        k = k_ref[:, blk]
        zero = jnp.zeros_like(q)
        s_ref[h % 2, 0] = _dot_nt(jnp.where(first_head, q, zero), k)
        s_ref[h % 2, 1] = _dot_nt(jnp.where(first_head, zero, q), k)

    def numerator(slot, j):
        m = jnp.max(s_ref[slot, j], axis=-1, keepdims=True)
        sums = []
        for rows in blocks:
            m_rows = m[rows]
            acc = None
            for c0 in range(0, s_ref.shape[-1], SOFTMAX_KEYS):
                keys = slice(c0, c0 + SOFTMAX_KEYS)
                e = jnp.exp2(s_ref[slot, j, rows, keys] - m_rows)
                for l0 in range(0, SOFTMAX_KEYS, LANES):
                    part = e[:, l0:l0 + LANES]
                    acc = part if acc is None else acc + part
                e_ref[j, rows, keys] = e.astype(BF16)
            sums.append(jnp.sum(acc, axis=-1, keepdims=True))
        return jnp.concatenate(sums, axis=0)

    scores(0)
    for h in range(DIFF_HEADS):
        if h + 1 < DIFF_HEADS:
            scores(h + 1)
        blk = slice(h * LANES, (h + 1) * LANES)
        l1 = numerator(h % 2, 0)
        l2 = numerator(h % 2, 1)
        c = (lam * l1 / l2).astype(BF16)
        for rows in blocks:
            e_ref[0, rows, :] = e_ref[0, rows, :] - c[rows] * e_ref[1, rows, :]
        o = _dot(e_ref[0], v_ref[:, blk]) / l1
        o_ref[:, blk] = (o * lax.rsqrt(jnp.mean(o * o, axis=-1, keepdims=True) + SUBLN_EPS) * g).astype(o_ref.dtype)


def _diff_attention(q, k, v, lam_params, subln, lam_init, batch, seq):
    tq = ATTN_Q_TILE
    q3, k3, v3 = (a.reshape(batch, seq, D_MODEL) for a in (q, k, v))
    q_spec = pl.BlockSpec((None, tq, D_MODEL), lambda b, i: (b, i, 0))
    kv_spec = pl.BlockSpec((None, seq, D_MODEL), lambda b, i: (b, 0, 0))
    lam_tile = jnp.pad(jnp.stack(lam_params).astype(F32), ((0, 4), (0, LANES - DIFF_HD)))
    out = pl.pallas_call(
        functools.partial(_diff_attn_kernel, lam_init=lam_init),
        grid=(batch, seq // tq),
        in_specs=[pl.BlockSpec((8, LANES), lambda b, i: (0, 0)), q_spec, kv_spec, kv_spec,
                  pl.BlockSpec((1, LANES), lambda b, i: (0, 0))],
        out_specs=q_spec,
        out_shape=jax.ShapeDtypeStruct((batch, seq, D_MODEL), BF16),
        scratch_shapes=[pltpu.VMEM((2, 2, tq, seq), F32), pltpu.VMEM((2, tq, seq), BF16)],
        compiler_params=_params("arbitrary", "arbitrary"),
        name="diff_attention",
    )(lam_tile, q3, k3, v3, subln[None, :])
    return out.reshape(batch * seq, D_MODEL)


def _mem_kv_kernel(mem_ref, g_ref, wkv_ref, kt_ref, v_ref):
    mn = _rms(mem_ref[...], g_ref[...], EPS).astype(BF16)
    for layer in range(wkv_ref.shape[0]):
        kv = _dot(mn, wkv_ref[layer])
        kt_ref[layer] = kv[:, :D_MODEL].T.astype(BF16)
        v_ref[layer] = kv[:, D_MODEL:].astype(BF16)


def _mem_kv(mem, mem_norm, xa_wkv):
    batch, n_mem, _ = mem.shape
    depth = xa_wkv.shape[0]
    w = xa_wkv.astype(BF16)
    return pl.pallas_call(
        _mem_kv_kernel,
        grid=(batch,),
        in_specs=[pl.BlockSpec((None, n_mem, D_MODEL), lambda b: (b, 0, 0)), _const_spec((1, D_MODEL)),
                  _const_spec(w.shape)],
        out_specs=[pl.BlockSpec((depth, None, D_MODEL, n_mem), lambda b: (0, b, 0, 0)),
                   pl.BlockSpec((depth, None, n_mem, D_MODEL), lambda b: (0, b, 0, 0))],
        out_shape=[jax.ShapeDtypeStruct((depth, batch, D_MODEL, n_mem), BF16),
                   jax.ShapeDtypeStruct((depth, batch, n_mem, D_MODEL), BF16)],
        compiler_params=_params("arbitrary"),
        name="memory_kv",
    )(mem, mem_norm[None, :], w)


def _cross_mlp_kernel(x_ref, gc_ref, wq_ref, kt_ref, v_ref, wo_ref, gm_ref, wup_ref, wdn_ref, gf_ref, o_ref,
                      *, final_norm):
    x = x_ref[...]
    hn = _rms(x, gc_ref[...], EPS).astype(BF16)
    q = (_dot(hn, wq_ref[...]) * (X_HD ** -0.5 * LOG2E)).astype(BF16)
    heads = []
    for h in range(X_HEADS):
        blk = slice(h * X_HD, (h + 1) * X_HD)
        s = _dot(q[:, blk], kt_ref[blk, :])
        e = jnp.exp2(s - jnp.max(s, axis=-1, keepdims=True))
        l = jnp.sum(e, axis=-1, keepdims=True)
        heads.append((_dot(e.astype(BF16), v_ref[:, blk]) / l).astype(BF16))
    x = x + _dot(jnp.concatenate(heads, axis=1), wo_ref[...])
    hn = _rms(x, gm_ref[...], EPS).astype(BF16)
    u = jnp.maximum(_dot(hn, wup_ref[...]), 0.0)
    x = x + _dot((u * u).astype(BF16), wdn_ref[...])
    if final_norm:
        x = _rms(x, gf_ref[...], EPS)
    o_ref[...] = x


def _cross_mlp(x2d, g_cross, wq, kt, v, wo, g_mlp, w_up, w_down, g_final, seq, final_norm):
    rows = x2d.shape[0]
    tm = ROW_TILE
    tiles_per_seq = seq // tm
    n_mem = v.shape[1]
    row_spec = pl.BlockSpec((tm, D_MODEL), lambda i: (i, 0))
    vec_spec = _const_spec((1, D_MODEL))
    return pl.pallas_call(
        functools.partial(_cross_mlp_kernel, final_norm=final_norm),
        grid=(rows // tm,),
        in_specs=[row_spec, vec_spec, _const_spec((D_MODEL, D_MODEL)),
                  pl.BlockSpec((None, D_MODEL, n_mem), lambda i: (i // tiles_per_seq, 0, 0)),
                  pl.BlockSpec((None, n_mem, D_MODEL), lambda i: (i // tiles_per_seq, 0, 0)),
                  _const_spec((D_MODEL, D_MODEL)), vec_spec, _const_spec((D_MODEL, D_FF)),
                  _const_spec((D_FF, D_MODEL)), vec_spec],
        out_specs=row_spec,
        out_shape=jax.ShapeDtypeStruct((rows, D_MODEL), F32),
        compiler_params=_params("arbitrary"),
        name="cross_attn_mlp",
    )(x2d, g_cross[None, :], wq.astype(BF16), kt, v, wo.astype(BF16), g_mlp[None, :],
      w_up.astype(BF16), w_down.astype(BF16), g_final[None, :])


def kernel(x, mem, ev_w_in, ev_conv_w, ev_conv_b, hy_w1, hy_b1, hy_w2, hy_b2, hy_w3, hy_freq, hy_skip,
           mla_q_norm, mla_w_uq, mla_kv_norm, mla_w_ukv, ev_w_out, od_w_qkv, dif_lq1, dif_lk1, dif_lq2,
           dif_lk2, dif_subln, od_w_out, norm_mix, norm_cross, norm_mlp, xa_wq, xa_wkv, xa_wo, mlp_up,
           mlp_down, mem_norm, final_norm):
    batch, seq, d = x.shape
    depth = norm_mix.shape[0]
    assert d == D_MODEL and seq % ROW_TILE == 0 and seq % ATTN_Q_TILE == 0
    x2d = x.reshape(batch * seq, d)
    kt_all, v_all = _mem_kv(mem, mem_norm, xa_wkv)
    cmat, smat = _dft_matrices(seq)
    for i in range(depth):
        j = i // 2
        if i % 2 == 0:
            kre, kim = _hyena_spectra(seq, hy_w1[j], hy_b1[j], hy_w2[j], hy_b2[j], hy_w3[j], hy_freq[j],
                                      cmat, smat)
            hy, q, k, v = _even_in(x2d, norm_mix[i], ev_w_in[j], mla_q_norm[j], mla_w_uq[j],
                                   mla_kv_norm[j], mla_w_ukv[j], seq)
            z = _hyena(hy, ev_conv_w[j], ev_conv_b[j], kre, kim, hy_skip[j], cmat, smat, batch, seq)
            o = _attention(q, k, v, batch, seq, MLA_HEADS, sum_lane=MLA_V)
            w_out = ev_w_out[j]
            w_mla = w_out[HY_CH:].reshape(MLA_HEADS, MLA_V, d)
            w_mla = jnp.pad(w_mla, ((0, 0), (0, LANES - MLA_V), (0, 0))).reshape(MLA_HEADS * LANES, d)
            x2d = _proj_residual(x2d, [(z.reshape(batch * seq, HY_CH), w_out[:HY_CH].astype(BF16)),
                                       (o, w_mla.astype(BF16))])
        else:
            lam_init = 0.8 - 0.6 * math.exp(-0.3 * i)
            q, k, v = _odd_in(x2d, norm_mix[i], od_w_qkv[j], seq)
            o = _diff_attention(q, k, v, (dif_lq1[j], dif_lk1[j], dif_lq2[j], dif_lk2[j]), dif_subln[j],
                                lam_init, batch, seq)
            x2d = _proj_residual(x2d, [(o, od_w_out[j].astype(BF16))])
        x2d = _cross_mlp(x2d, norm_cross[i], xa_wq[i], kt_all[i], v_all[i], xa_wo[i], norm_mlp[i],
                         mlp_up[i], mlp_down[i], final_norm, seq, final_norm=(i == depth - 1))
    return x2d.reshape(batch, seq, d)
```

```python
import functools
import math

import jax
import jax.numpy as jnp
import numpy as np
from jax import lax
from jax.experimental import pallas as pl
from jax.experimental.pallas import tpu as pltpu

F32 = jnp.float32
BF16 = jnp.bfloat16

D_MODEL = 1024
EPS = 1e-6
ROPE_THETA = 10000.0
HY_CH = 512
HY_ORDER = 2
HY_EMB = 33
HY_BANDS = (HY_EMB - 1) // 2
HY_FFN = 64
HY_FAST_PCT = 0.3
HY_SLOW_PCT = 1.5
HY_TARGET = 1e-2
MLA_HEADS = 8
MLA_NOPE = 64
MLA_ROPE = 32
MLA_V = 64
MLA_QK = MLA_NOPE + MLA_ROPE
MLA_Q_RANK = D_MODEL // 4
MLA_KV_RANK = D_MODEL // 8
DIFF_HEADS = 8
DIFF_HD = D_MODEL // DIFF_HEADS // 2
SUBLN_EPS = 1e-5
X_HEADS = 4
X_HD = D_MODEL // X_HEADS
D_FF = 4 * D_MODEL

LOG2E = math.log2(math.e)
LANES = 128
V7X_VMEM_BYTES = 64 * 1024 * 1024
VMEM_LIMIT = V7X_VMEM_BYTES - 8 * 1024 * 1024

ROW_TILE = 1024
MLP_ROW_TILE = 512
ROW_SUBTILE = 256
ATTN_Q_TILE = 512
SOFTMAX_ROWS = 16
SOFTMAX_KEYS = 512
HY_CH_TILE = 256
HY_ROW_CHUNK = 512


def _const_spec(shape):
    nd = len(shape)
    return pl.BlockSpec(shape, lambda *_: (0,) * nd, pipeline_mode=pl.Buffered(1))


def _params(*sem):
    return pltpu.CompilerParams(dimension_semantics=sem, vmem_limit_bytes=VMEM_LIMIT)


def _rms(x, g, eps):
    return x * lax.rsqrt(jnp.mean(x * x, axis=-1, keepdims=True) + eps) * g


def _rope(x, c, s_lo, s_hi, half):
    return x * c + pltpu.roll(x, LANES - half, 1) * s_lo + pltpu.roll(x, half, 1) * s_hi


def _dot(a, b):
    return jnp.dot(a, b, preferred_element_type=F32)


def _dot_nt(a, b):
    return lax.dot_general(a, b, (((1,), (1,)), ((), ())), preferred_element_type=F32)


def _dot_f32(a, b):
    return jnp.dot(a, b, preferred_element_type=F32, precision=lax.Precision.HIGHEST)


def _filter_kernel(feats_ref, w1_ref, b1_ref, w2_ref, b2_ref, fr_ref, w3f_ref, w3b_ref, t_ref, dl_ref,
                   cw_ref, sw_ref, cm_ref, sm_ref, kre_ref, kim_ref, act_ref, *, inv_scale):
    @pl.when((pl.program_id(0) == 0) & (pl.program_id(1) == 0))
    def _():
        fr = fr_ref[...]
        a = jnp.sin(fr * (_dot_f32(feats_ref[...], w1_ref[...]) + b1_ref[...]))
        act_ref[...] = jnp.sin(fr * (_dot_f32(a, w2_ref[...]) + b2_ref[...]))

    a = act_ref[...]
    window = jnp.exp(-t_ref[...] * jnp.abs(dl_ref[...]))
    h_f = _dot_f32(a, w3f_ref[...]) * window
    h_b = _dot_f32(a, w3b_ref[...]) * window
    row = lax.broadcasted_iota(jnp.int32, h_b.shape, 0)
    h_b = jnp.where(row == 0, 0.0, h_b)
    h = jnp.concatenate([h_f + h_b, h_f - h_b], axis=1)
    hi = h.astype(BF16)
    lo = (h - hi.astype(F32)).astype(BF16)
    tc = h_f.shape[1]
    for r in range(0, h.shape[0], HY_ROW_CHUNK):
        rows = slice(r, r + HY_ROW_CHUNK)
        c = _dot(cm_ref[rows, :], hi) + _dot(cm_ref[rows, :], lo)
        s = _dot(sm_ref[rows, :], hi) + _dot(sm_ref[rows, :], lo)
        cw = cw_ref[rows, :]
        sw = sw_ref[rows, :]
        kre_ref[rows, :] = (cw * c[:, :tc] + sw * s[:, :tc]) * inv_scale
        kim_ref[rows, :] = (sw * c[:, tc:] - cw * s[:, tc:]) * inv_scale


def _hyena_spectra(seq, hy_w1, hy_b1, hy_w2, hy_b2, hy_w3, hy_freq, cmat, smat):
    f32 = F32
    n = 2 * seq
    t = jnp.linspace(0.0, 1.0, seq, dtype=f32)[:, None]
    w = (2.0 * math.pi / seq) * jnp.arange(seq, dtype=f32)[:, None]
    bands = jnp.linspace(1e-4, HY_BANDS - 1, HY_BANDS, dtype=f32)[None, :]
    feats = jnp.concatenate([t, jnp.cos(bands * w), -jnp.sin(bands * w)], axis=-1)
    feats = jnp.pad(feats, ((0, 0), (0, LANES - HY_EMB)))
    pad_f = LANES - HY_FFN
    w1 = jnp.pad(hy_w1.astype(f32), ((0, LANES - HY_EMB), (0, pad_f)))
    b1 = jnp.pad(hy_b1.astype(f32), (0, pad_f))[None, :]
    w2 = jnp.pad(hy_w2.astype(f32), ((0, pad_f), (0, pad_f)))
    b2 = jnp.pad(hy_b2.astype(f32), (0, pad_f))[None, :]
    fr = jnp.pad(hy_freq.astype(f32), (0, pad_f))[None, :]
    w3 = jnp.pad(hy_w3.astype(f32), ((0, pad_f), (0, 0))).reshape(LANES, HY_ORDER, 2, HY_CH)
    w3f = w3[:, :, 0, :].reshape(LANES, HY_ORDER * HY_CH)
    w3b = w3[:, :, 1, :].reshape(LANES, HY_ORDER * HY_CH)
    max_decay = math.log(HY_TARGET) / HY_FAST_PCT
    min_decay = math.log(HY_TARGET) / HY_SLOW_PCT
    deltas = jnp.linspace(min_decay, max_decay, HY_CH, dtype=f32)[None, :]
    half_w = (math.pi / n) * (jnp.arange(seq, dtype=f32) + 0.5)
    cw = jnp.cos(half_w)[:, None]
    sw = jnp.sin(half_w)[:, None]

    tc = HY_CH_TILE
    ncb = HY_CH // tc
    small = lambda shape: pl.BlockSpec(shape, lambda o, c: (0,) * len(shape))
    out = pl.pallas_call(
        functools.partial(_filter_kernel, inv_scale=2.0 / n),
        grid=(HY_ORDER, ncb),
        in_specs=[
            small((seq, LANES)), small((LANES, LANES)), small((1, LANES)), small((LANES, LANES)),
            small((1, LANES)), small((1, LANES)),
            pl.BlockSpec((LANES, tc), lambda o, c: (0, o * ncb + c)),
            pl.BlockSpec((LANES, tc), lambda o, c: (0, o * ncb + c)),
            small((seq, 1)),
            pl.BlockSpec((1, tc), lambda o, c: (0, c)),
            small((seq, 1)), small((seq, 1)),
            _const_spec((seq, seq)), _const_spec((seq, seq)),
        ],
        out_specs=[pl.BlockSpec((None, seq, tc), lambda o, c: (o, 0, c))] * 2,
        out_shape=[jax.ShapeDtypeStruct((HY_ORDER, seq, HY_CH), f32)] * 2,
        scratch_shapes=[pltpu.VMEM((seq, LANES), f32)],
        compiler_params=_params("arbitrary", "arbitrary"),
        name="hyena_filter_spectra",
    )(feats, w1, b1, w2, b2, fr, w3f, w3b, t, deltas, cw, sw, cmat, smat)
    return out


def _dft_matrices(seq):
    n4 = 8 * seq
    blk = 64
    odd_t = 2 * jnp.arange(seq, dtype=jnp.int32) + 1

    def table(rows):
        ang = ((rows[:, None] * odd_t[None, :]) % n4).astype(F32) * (2.0 * math.pi / n4)
        return jnp.cos(ang), jnp.sin(ang)

    ca, sa = (a[:, None, :] for a in table(2 * blk * jnp.arange(seq // blk, dtype=jnp.int32)))
    cb, sb = (a[None, :, :] for a in table(2 * jnp.arange(blk, dtype=jnp.int32) + 1))
    cmat = (ca * cb - sa * sb).reshape(seq, seq)
    smat = (sa * cb + ca * sb).reshape(seq, seq)
    return cmat.astype(BF16), smat.astype(BF16)


def _hyena_kernel(v_ref, x1_ref, x2_ref, wv_ref, wx1_ref, wx2_ref, bv_ref, bx1_ref, bx2_ref,
                  kre_ref, kim_ref, skip_ref, cm_ref, sm_ref, o_ref, z_ref, yre_ref, yim_ref):
    seq = v_ref.shape[0]
    row = lax.broadcasted_iota(jnp.int32, v_ref.shape, 0)
    first = row == 0
    last = row == seq - 1

    def short_conv(u_ref, w_ref, b_ref):
        u = u_ref[...].astype(F32)
        w = w_ref[...]
        prev = jnp.where(first, 0.0, pltpu.roll(u, 1, 0))
        nxt = jnp.where(last, 0.0, pltpu.roll(u, seq - 1, 0))
        return prev * w[0:1] + u * w[1:2] + nxt * w[2:3] + b_ref[...]

    chunks = [slice(r, r + HY_ROW_CHUNK) for r in range(0, seq, HY_ROW_CHUNK)]
    z_ref[...] = short_conv(v_ref, wv_ref, bv_ref)
    gate_refs = ((x1_ref, wx1_ref, bx1_ref), (x2_ref, wx2_ref, bx2_ref))
    skip = skip_ref[...]
    for o in range(HY_ORDER):
        zb = z_ref[...].astype(BF16)
        for rows in chunks:
            az = _dot(cm_ref[rows, :], zb)
            bz = _dot(sm_ref[rows, :], zb)
            kre = kre_ref[o, rows, :]
            kim = kim_ref[o, rows, :]
            yre_ref[rows, :] = (az * kre + bz * kim).astype(BF16)
            yim_ref[rows, :] = (bz * kre - az * kim).astype(BF16)
        gate = short_conv(*gate_refs[o])
        yre = yre_ref[...]
        yim = yim_ref[...]
        for rows in chunks:
            y = _dot(cm_ref[rows, :], yre) + _dot(sm_ref[rows, :], yim)
            z_new = gate[rows] * (y + skip[o:o + 1] * z_ref[rows, :])
            if o == HY_ORDER - 1:
                o_ref[rows, :] = z_new.astype(o_ref.dtype)
            else:
                z_ref[rows, :] = z_new


def _hyena(hy, conv_w, conv_b, kre, kim, skip, cmat, smat, batch, seq):
    tc = HY_CH_TILE
    ncb = HY_CH // tc
    hy3 = hy.reshape(batch, seq, 3 * HY_CH)
    cb = conv_b[None, :]
    u_spec = lambda g: pl.BlockSpec((None, seq, tc), lambda c, b: (b, 0, g * ncb + c))
    w_spec = lambda g: pl.BlockSpec((3, tc), lambda c, b: (0, g * ncb + c))
    b_spec = lambda g: pl.BlockSpec((1, tc), lambda c, b: (0, g * ncb + c))
    k_spec = pl.BlockSpec((HY_ORDER, seq, tc), lambda c, b: (0, 0, c), pipeline_mode=pl.Buffered(1))
    return pl.pallas_call(
        _hyena_kernel,
        grid=(ncb, batch),
        in_specs=[u_spec(0), u_spec(1), u_spec(2), w_spec(0), w_spec(1), w_spec(2),
                  b_spec(0), b_spec(1), b_spec(2), k_spec, k_spec,
                  pl.BlockSpec((HY_ORDER, tc), lambda c, b: (0, c)),
                  _const_spec((seq, seq)), _const_spec((seq, seq))],
        out_specs=pl.BlockSpec((None, seq, tc), lambda c, b: (b, 0, c)),
        out_shape=jax.ShapeDtypeStruct((batch, seq, HY_CH), BF16),
        scratch_shapes=[pltpu.VMEM((seq, tc), F32), pltpu.VMEM((seq, tc), BF16), pltpu.VMEM((seq, tc), BF16)],
        compiler_params=_params("arbitrary", "arbitrary"),
        name="hyena_long_conv",
    )(hy3, hy3, hy3, conv_w, conv_w, conv_w, cb, cb, cb, kre, kim, skip, cmat, smat)


def _even_in_kernel(x_ref, g_ref, win_ref, qg_ref, wuq_ref, kvg_ref, wuk_ref, wuv_ref, ones_ref,
                    c_ref, slo_ref, shi_ref, hy_ref, q_ref, k_ref, v_ref):
    c0 = 3 * HY_CH
    c1 = c0 + MLA_Q_RANK
    c2 = c1 + MLA_KV_RANK
    half = MLA_ROPE // 2
    scale = MLA_QK ** -0.5 * LOG2E
    for r in range(0, x_ref.shape[0], ROW_SUBTILE):
        rows = slice(r, r + ROW_SUBTILE)
        xn = _rms(x_ref[rows, :], g_ref[...], EPS).astype(BF16)
        proj = _dot(xn, win_ref[...])
        hy_ref[rows, :] = proj[:, :c0].astype(hy_ref.dtype)
        c = c_ref[rows, :]
        s_lo = slo_ref[rows, :]
        s_hi = shi_ref[rows, :]
        qn = _rms(proj[:, c0:c1], qg_ref[...], EPS).astype(BF16)
        q = _dot(qn, wuq_ref[...])
        kvn = _rms(proj[:, c1:c2], kvg_ref[...], EPS).astype(BF16)
        kn = _dot(kvn, wuk_ref[...])
        v_ref[rows, :] = (_dot(kvn, wuv_ref[...]) + ones_ref[...]).astype(BF16)
        k_pe = _rope(pltpu.roll(proj[:, c2:c2 + LANES], MLA_NOPE, 1), c, s_lo, s_hi, half)
        for h in range(MLA_HEADS):
            blk = slice(h * LANES, (h + 1) * LANES)
            q_ref[rows, blk] = (_rope(q[:, blk], c, s_lo, s_hi, half) * scale).astype(BF16)
            k_ref[rows, blk] = (kn[:, blk] + k_pe).astype(BF16)


def _mla_rope_tables(seq):
    inv = ROPE_THETA ** (-jnp.arange(0, MLA_ROPE, 2, dtype=F32) / MLA_ROPE)
    ang = jnp.arange(seq, dtype=F32)[:, None] * inv[None, :]
    cos, sin = jnp.cos(ang), jnp.sin(ang)
    half = MLA_ROPE // 2
    one = jnp.ones((seq, MLA_NOPE), F32)
    zero = jnp.zeros((seq, MLA_NOPE), F32)
    tail1 = jnp.ones((seq, LANES - MLA_QK), F32)
    tail0 = jnp.zeros((seq, LANES - MLA_QK), F32)
    zh = jnp.zeros((seq, half), F32)
    c = jnp.concatenate([one, cos, cos, tail1], axis=1)
    s_lo = jnp.concatenate([zero, -sin, zh, tail0], axis=1)
    s_hi = jnp.concatenate([zero, zh, sin, tail0], axis=1)
    return c, s_lo, s_hi


def _pad_heads(w, heads, width):
    rows = w.shape[0]
    w = w.reshape(rows, heads, width)
    return jnp.pad(w, ((0, 0), (0, 0), (0, LANES - width))).reshape(rows, heads * LANES)


def _even_in(x2d, g, w_in, q_norm, w_uq, kv_norm, w_ukv, seq):
    rows = x2d.shape[0]
    tm = ROW_TILE
    n_in = w_in.shape[1]
    n_pad = -(-(n_in + LANES - MLA_ROPE) // LANES) * LANES
    win = jnp.pad(w_in, ((0, 0), (0, n_pad - n_in))).astype(BF16)
    wuq = _pad_heads(w_uq, MLA_HEADS, MLA_QK).astype(BF16)
    wkv = w_ukv.reshape(MLA_KV_RANK, MLA_HEADS, MLA_NOPE + MLA_V)
    wuk = _pad_heads(wkv[:, :, :MLA_NOPE].reshape(MLA_KV_RANK, -1), MLA_HEADS, MLA_NOPE).astype(BF16)
    wuv = _pad_heads(wkv[:, :, MLA_NOPE:].reshape(MLA_KV_RANK, -1), MLA_HEADS, MLA_V).astype(BF16)
    c, s_lo, s_hi = _mla_rope_tables(seq)
    tiles_per_seq = seq // tm
    row_spec = lambda width: pl.BlockSpec((tm, width), lambda i: (i, 0))
    tab_spec = pl.BlockSpec((tm, LANES), lambda i: (i % tiles_per_seq, 0))
    hw = MLA_HEADS * LANES
    ones_col = jnp.tile((jnp.arange(LANES) == MLA_V).astype(F32), MLA_HEADS)[None, :]
    return pl.pallas_call(
        _even_in_kernel,
        grid=(rows // tm,),
        in_specs=[row_spec(D_MODEL), _const_spec((1, D_MODEL)), _const_spec(win.shape),
                  _const_spec((1, MLA_Q_RANK)), _const_spec(wuq.shape),
                  _const_spec((1, MLA_KV_RANK)), _const_spec(wuk.shape), _const_spec(wuv.shape),
                  _const_spec((1, hw)), tab_spec, tab_spec, tab_spec],
        out_specs=[row_spec(3 * HY_CH), row_spec(hw), row_spec(hw), row_spec(hw)],
        out_shape=[jax.ShapeDtypeStruct((rows, 3 * HY_CH), BF16)] + [jax.ShapeDtypeStruct((rows, hw), BF16)] * 3,
        compiler_params=_params("arbitrary"),
        name="even_in_proj",
    )(x2d, g[None, :], win, q_norm[None, :], wuq, kv_norm[None, :], wuk, wuv, ones_col, c, s_lo, s_hi)


def _attn_kernel(q_ref, k_ref, v_ref, o_ref, *, heads, sum_lane):
    for h in range(heads):
        blk = slice(h * LANES, (h + 1) * LANES)
        s = _dot_nt(q_ref[:, blk], k_ref[:, blk])
        e = jnp.exp2(s - jnp.max(s, axis=-1, keepdims=True)).astype(BF16)
        o = _dot(e, v_ref[:, blk])
        l = o[:, sum_lane:sum_lane + 1]
        o_ref[:, blk] = (o / l).astype(o_ref.dtype)


def _attention(q, k, v, batch, seq, heads, sum_lane):
    tq = ATTN_Q_TILE
    hw = heads * LANES
    q3, k3, v3 = (a.reshape(batch, seq, hw) for a in (q, k, v))
    q_spec = pl.BlockSpec((None, tq, hw), lambda b, i: (b, i, 0))
    kv_spec = pl.BlockSpec((None, seq, hw), lambda b, i: (b, 0, 0))
    out = pl.pallas_call(
        functools.partial(_attn_kernel, heads=heads, sum_lane=sum_lane),
        grid=(batch, seq // tq),
        in_specs=[q_spec, kv_spec, kv_spec],
        out_specs=q_spec,
        out_shape=jax.ShapeDtypeStruct((batch, seq, hw), BF16),
        compiler_params=_params("arbitrary", "arbitrary"),
        name="mla_attention",
    )(q3, k3, v3)
    return out.reshape(batch * seq, hw)


def _odd_in_kernel(x_ref, g_ref, w_ref, c_ref, slo_ref, shi_ref, q_ref, k_ref, v_ref):
    half = DIFF_HD // 2
    scale = DIFF_HD ** -0.5 * LOG2E
    for r in range(0, x_ref.shape[0], ROW_SUBTILE):
        rows = slice(r, r + ROW_SUBTILE)
        xn = _rms(x_ref[rows, :], g_ref[...], EPS).astype(BF16)
        qkv = _dot(xn, w_ref[...])
        c = c_ref[rows, :]
        s_lo = slo_ref[rows, :]
        s_hi = shi_ref[rows, :]
        for h in range(DIFF_HEADS):
            blk = slice(h * LANES, (h + 1) * LANES)
            kblk = slice(D_MODEL + h * LANES, D_MODEL + (h + 1) * LANES)
            q_ref[rows, blk] = (_rope(qkv[:, blk], c, s_lo, s_hi, half) * scale).astype(BF16)
            k_ref[rows, blk] = _rope(qkv[:, kblk], c, s_lo, s_hi, half).astype(BF16)
        v_ref[rows, :] = qkv[:, 2 * D_MODEL:].astype(BF16)


def _diff_rope_tables(seq):
    inv = ROPE_THETA ** (-jnp.arange(0, DIFF_HD, 2, dtype=F32) / DIFF_HD)
    ang = jnp.arange(seq, dtype=F32)[:, None] * inv[None, :]
    cos, sin = jnp.cos(ang), jnp.sin(ang)
    zero = jnp.zeros_like(sin)
    reps = LANES // DIFF_HD
    c = jnp.tile(jnp.concatenate([cos, cos], axis=1), (1, reps))
    s_lo = jnp.tile(jnp.concatenate([-sin, zero], axis=1), (1, reps))
    s_hi = jnp.tile(jnp.concatenate([zero, sin], axis=1), (1, reps))
    return c, s_lo, s_hi


def _odd_in(x2d, g, w_qkv, seq):
    rows = x2d.shape[0]
    tm = ROW_TILE
    c, s_lo, s_hi = _diff_rope_tables(seq)
    tiles_per_seq = seq // tm
    row_spec = lambda width: pl.BlockSpec((tm, width), lambda i: (i, 0))
    tab_spec = pl.BlockSpec((tm, LANES), lambda i: (i % tiles_per_seq, 0))
    w = w_qkv.astype(BF16)
    return pl.pallas_call(
        _odd_in_kernel,
        grid=(rows // tm,),
        in_specs=[row_spec(D_MODEL), _const_spec((1, D_MODEL)), _const_spec(w.shape),
                  tab_spec, tab_spec, tab_spec],
        out_specs=[row_spec(D_MODEL)] * 3,
        out_shape=[jax.ShapeDtypeStruct((rows, D_MODEL), BF16)] * 3,
        compiler_params=_params("arbitrary"),
        name="odd_in_proj",
    )(x2d, g[None, :], w, c, s_lo, s_hi)


def _diff_attn_kernel(lam_ref, q_ref, k_ref, v_ref, g_ref, o_ref, s_ref, e_ref, *, lam_init):
    lp = lam_ref[...]
    lam = (jnp.exp(jnp.sum(lp[0:1] * lp[1:2], axis=-1, keepdims=True))
           - jnp.exp(jnp.sum(lp[2:3] * lp[3:4], axis=-1, keepdims=True)) + lam_init)
    g = g_ref[...] * (1.0 - lam_init)
    tq = q_ref.shape[0]
    first_head = lax.broadcasted_iota(jnp.int32, (tq, LANES), 1) < DIFF_HD
    blocks = [slice(r, r + SOFTMAX_ROWS) for r in range(0, tq, SOFTMAX_ROWS)]

    def scores(h):
        blk = slice(h * LANES, (h + 1) * LANES)
        q = q_ref[:, blk]
        k = k_ref[:, blk]
        zero = jnp.zeros_like(q)
        s_ref[h % 2, 0] = _dot_nt(jnp.where(first_head, q, zero), k)
        s_ref[h % 2, 1] = _dot_nt(jnp.where(first_head, zero, q), k)

    def numerator(slot, j):
        m = jnp.max(s_ref[slot, j], axis=-1, keepdims=True)
        sums = []
        for rows in blocks:
            m_rows = m[rows]
            acc = None
            for c0 in range(0, s_ref.shape[-1], SOFTMAX_KEYS):
                keys = slice(c0, c0 + SOFTMAX_KEYS)
                e = jnp.exp2(s_ref[slot, j, rows, keys] - m_rows)
                for l0 in range(0, SOFTMAX_KEYS, LANES):
                    part = e[:, l0:l0 + LANES]
                    acc = part if acc is None else acc + part
                e_ref[j, rows, keys] = e.astype(BF16)
            sums.append(jnp.sum(acc, axis=-1, keepdims=True))
        return jnp.concatenate(sums, axis=0)

    scores(0)
    for h in range(DIFF_HEADS):
        if h + 1 < DIFF_HEADS:
            scores(h + 1)
        blk = slice(h * LANES, (h + 1) * LANES)
        l1 = numerator(h % 2, 0)
        l2 = numerator(h % 2, 1)
        c = (lam * l1 / l2).astype(BF16)
        for rows in blocks:
            e_ref[0, rows, :] = e_ref[0, rows, :] - c[rows] * e_ref[1, rows, :]
        o = _dot(e_ref[0], v_ref[:, blk]) / l1
        o_ref[:, blk] = (o * lax.rsqrt(jnp.mean(o * o, axis=-1, keepdims=True) + SUBLN_EPS) * g).astype(o_ref.dtype)


def _diff_attention(q, k, v, lam_params, subln, lam_init, batch, seq):
    tq = ATTN_Q_TILE
    q3, k3, v3 = (a.reshape(batch, seq, D_MODEL) for a in (q, k, v))
    q_spec = pl.BlockSpec((None, tq, D_MODEL), lambda b, i: (b, i, 0))
    kv_spec = pl.BlockSpec((None, seq, D_MODEL), lambda b, i: (b, 0, 0))
    lam_tile = jnp.pad(jnp.stack(lam_params).astype(F32), ((0, 4), (0, LANES - DIFF_HD)))
    out = pl.pallas_call(
        functools.partial(_diff_attn_kernel, lam_init=lam_init),
        grid=(batch, seq // tq),
        in_specs=[pl.BlockSpec((8, LANES), lambda b, i: (0, 0)), q_spec, kv_spec, kv_spec,
                  pl.BlockSpec((1, LANES), lambda b, i: (0, 0))],
        out_specs=q_spec,
        out_shape=jax.ShapeDtypeStruct((batch, seq, D_MODEL), BF16),
        scratch_shapes=[pltpu.VMEM((2, 2, tq, seq), F32), pltpu.VMEM((2, tq, seq), BF16)],
        compiler_params=_params("arbitrary", "arbitrary"),
        name="diff_attention",
    )(lam_tile, q3, k3, v3, subln[None, :])
    return out.reshape(batch * seq, D_MODEL)


def _mem_kv_kernel(mem_ref, g_ref, wkv_ref, kt_ref, v_ref):
    mn = _rms(mem_ref[...], g_ref[...], EPS).astype(BF16)
    for layer in range(wkv_ref.shape[0]):
        kv = _dot(mn, wkv_ref[layer])
        kt_ref[layer] = kv[:, :D_MODEL].T.astype(BF16)
        v_ref[layer] = kv[:, D_MODEL:].astype(BF16)


def _mem_kv(mem, mem_norm, xa_wkv):
    batch, n_mem, _ = mem.shape
    depth = xa_wkv.shape[0]
    w = xa_wkv.astype(BF16)
    return pl.pallas_call(
        _mem_kv_kernel,
        grid=(batch,),
        in_specs=[pl.BlockSpec((None, n_mem, D_MODEL), lambda b: (b, 0, 0)), _const_spec((1, D_MODEL)),
                  _const_spec(w.shape)],
        out_specs=[pl.BlockSpec((depth, None, D_MODEL, n_mem), lambda b: (0, b, 0, 0)),
                   pl.BlockSpec((depth, None, n_mem, D_MODEL), lambda b: (0, b, 0, 0))],
        out_shape=[jax.ShapeDtypeStruct((depth, batch, D_MODEL, n_mem), BF16),
                   jax.ShapeDtypeStruct((depth, batch, n_mem, D_MODEL), BF16)],
        compiler_params=_params("arbitrary"),
        name="memory_kv",
    )(mem, mem_norm[None, :], w)


def _cross_mlp_kernel(*refs, n_mix, final_norm):
    x_ref, o_ref = refs[0], refs[-1]
    mix = refs[1:1 + 2 * n_mix]
    gc_ref, wq_ref, kt_ref, v_ref, wo_ref, gm_ref, wup_ref, wdn_ref, gf_ref = refs[1 + 2 * n_mix:-1]
    x = x_ref[...]
    for a_ref, w_ref in zip(mix[0::2], mix[1::2]):
        x = x + _dot(a_ref[...], w_ref[...])
    hn = _rms(x, gc_ref[...], EPS).astype(BF16)
    q = (_dot(hn, wq_ref[...]) * (X_HD ** -0.5 * LOG2E)).astype(BF16)
    heads = []
    for h in range(X_HEADS):
        blk = slice(h * X_HD, (h + 1) * X_HD)
        s = _dot(q[:, blk], kt_ref[blk, :])
        e = jnp.exp2(s - jnp.max(s, axis=-1, keepdims=True))
        l = jnp.sum(e, axis=-1, keepdims=True)
        heads.append((_dot(e.astype(BF16), v_ref[:, blk]) / l).astype(BF16))
    x = x + _dot(jnp.concatenate(heads, axis=1), wo_ref[...])
    hn = _rms(x, gm_ref[...], EPS).astype(BF16)
    u = jnp.maximum(_dot(hn, wup_ref[...]), 0.0)
    x = x + _dot((u * u).astype(BF16), wdn_ref[...])
    if final_norm:
        x = _rms(x, gf_ref[...], EPS)
    o_ref[...] = x


def _mix_cross_mlp(x2d, mix_pairs, g_cross, wq, kt, v, wo, g_mlp, w_up, w_down, g_final, seq, final_norm):
    rows = x2d.shape[0]
    tm = MLP_ROW_TILE
    tiles_per_seq = seq // tm
    n_mem = v.shape[1]
    row_spec = lambda width: pl.BlockSpec((tm, width), lambda i: (i, 0))
    vec_spec = _const_spec((1, D_MODEL))
    mix_specs, mix_args = [], []
    for a, w in mix_pairs:
        mix_specs += [row_spec(a.shape[1]), _const_spec(w.shape)]
        mix_args += [a, w]
    return pl.pallas_call(
        functools.partial(_cross_mlp_kernel, n_mix=len(mix_pairs), final_norm=final_norm),
        grid=(rows // tm,),
        in_specs=[row_spec(D_MODEL)] + mix_specs + [
            vec_spec, _const_spec((D_MODEL, D_MODEL)),
            pl.BlockSpec((None, D_MODEL, n_mem), lambda i: (i // tiles_per_seq, 0, 0)),
            pl.BlockSpec((None, n_mem, D_MODEL), lambda i: (i // tiles_per_seq, 0, 0)),
            _const_spec((D_MODEL, D_MODEL)), vec_spec, _const_spec((D_MODEL, D_FF)),
            _const_spec((D_FF, D_MODEL)), vec_spec],
        out_specs=row_spec(D_MODEL),
        out_shape=jax.ShapeDtypeStruct((rows, D_MODEL), F32),
        compiler_params=_params("arbitrary"),
        name="mix_cross_mlp",
    )(x2d, *mix_args, g_cross[None, :], wq.astype(BF16), kt, v, wo.astype(BF16), g_mlp[None, :],
      w_up.astype(BF16), w_down.astype(BF16), g_final[None, :])


def kernel(x, mem, ev_w_in, ev_conv_w, ev_conv_b, hy_w1, hy_b1, hy_w2, hy_b2, hy_w3, hy_freq, hy_skip,
           mla_q_norm, mla_w_uq, mla_kv_norm, mla_w_ukv, ev_w_out, od_w_qkv, dif_lq1, dif_lk1, dif_lq2,
           dif_lk2, dif_subln, od_w_out, norm_mix, norm_cross, norm_mlp, xa_wq, xa_wkv, xa_wo, mlp_up,
           mlp_down, mem_norm, final_norm):
    batch, seq, d = x.shape
    depth = norm_mix.shape[0]
    assert d == D_MODEL and seq % ROW_TILE == 0 and seq % MLP_ROW_TILE == 0 and seq % ATTN_Q_TILE == 0
    x2d = x.reshape(batch * seq, d)
    kt_all, v_all = _mem_kv(mem, mem_norm, xa_wkv)
    cmat, smat = _dft_matrices(seq)
    for i in range(depth):
        j = i // 2
        if i % 2 == 0:
            kre, kim = _hyena_spectra(seq, hy_w1[j], hy_b1[j], hy_w2[j], hy_b2[j], hy_w3[j], hy_freq[j],
                                      cmat, smat)
            hy, q, k, v = _even_in(x2d, norm_mix[i], ev_w_in[j], mla_q_norm[j], mla_w_uq[j],
                                   mla_kv_norm[j], mla_w_ukv[j], seq)
            z = _hyena(hy, ev_conv_w[j], ev_conv_b[j], kre, kim, hy_skip[j], cmat, smat, batch, seq)
            o = _attention(q, k, v, batch, seq, MLA_HEADS, sum_lane=MLA_V)
            w_out = ev_w_out[j]
            w_mla = w_out[HY_CH:].reshape(MLA_HEADS, MLA_V, d)
            w_mla = jnp.pad(w_mla, ((0, 0), (0, LANES - MLA_V), (0, 0))).reshape(MLA_HEADS * LANES, d)
            mix = [(z.reshape(batch * seq, HY_CH), w_out[:HY_CH].astype(BF16)), (o, w_mla.astype(BF16))]
        else:
            lam_init = 0.8 - 0.6 * math.exp(-0.3 * i)
            q, k, v = _odd_in(x2d, norm_mix[i], od_w_qkv[j], seq)
            o = _diff_attention(q, k, v, (dif_lq1[j], dif_lk1[j], dif_lq2[j], dif_lk2[j]), dif_subln[j],
                                lam_init, batch, seq)
            mix = [(o, od_w_out[j].astype(BF16))]
        x2d = _mix_cross_mlp(x2d, mix, norm_cross[i], xa_wq[i], kt_all[i], v_all[i], xa_wo[i], norm_mlp[i],
                             mlp_up[i], mlp_down[i], final_norm, seq, final_norm=(i == depth - 1))
    return x2d.reshape(batch, seq, d)
```

```python
import functools
import math

import jax
import jax.numpy as jnp
import numpy as np
from jax import lax
from jax.experimental import pallas as pl
from jax.experimental.pallas import tpu as pltpu

F32 = jnp.float32
BF16 = jnp.bfloat16

D_MODEL = 1024
EPS = 1e-6
ROPE_THETA = 10000.0
HY_CH = 512
HY_ORDER = 2
HY_EMB = 33
HY_BANDS = (HY_EMB - 1) // 2
HY_FFN = 64
HY_FAST_PCT = 0.3
HY_SLOW_PCT = 1.5
HY_TARGET = 1e-2
MLA_HEADS = 8
MLA_NOPE = 64
MLA_ROPE = 32
MLA_V = 64
MLA_QK = MLA_NOPE + MLA_ROPE
MLA_Q_RANK = D_MODEL // 4
MLA_KV_RANK = D_MODEL // 8
DIFF_HEADS = 8
DIFF_HD = D_MODEL // DIFF_HEADS // 2
SUBLN_EPS = 1e-5
X_HEADS = 4
X_HD = D_MODEL // X_HEADS
D_FF = 4 * D_MODEL

LOG2E = math.log2(math.e)
LANES = 128
V7X_VMEM_BYTES = 64 * 1024 * 1024
VMEM_LIMIT = V7X_VMEM_BYTES - 8 * 1024 * 1024

ROW_TILE = 1024
MLP_ROW_TILE = 512
ROW_SUBTILE = 256
ATTN_Q_TILE = 512
SOFTMAX_ROWS = 16
SOFTMAX_KEYS = 512
HY_CH_TILE = 256
HY_ROW_CHUNK = 512


def _const_spec(shape):
    nd = len(shape)
    return pl.BlockSpec(shape, lambda *_: (0,) * nd, pipeline_mode=pl.Buffered(1))


def _params(*sem):
    return pltpu.CompilerParams(dimension_semantics=sem, vmem_limit_bytes=VMEM_LIMIT)


def _rms(x, g, eps):
    return x * lax.rsqrt(jnp.mean(x * x, axis=-1, keepdims=True) + eps) * g


def _rope(x, c, s_lo, s_hi, half):
    return x * c + pltpu.roll(x, LANES - half, 1) * s_lo + pltpu.roll(x, half, 1) * s_hi


def _dot(a, b):
    return jnp.dot(a, b, preferred_element_type=F32)


def _dot_nt(a, b):
    return lax.dot_general(a, b, (((1,), (1,)), ((), ())), preferred_element_type=F32)


def _dot_f32(a, b):
    return jnp.dot(a, b, preferred_element_type=F32, precision=lax.Precision.HIGHEST)


def _filter_kernel(feats_ref, w1_ref, b1_ref, w2_ref, b2_ref, fr_ref, w3f_ref, w3b_ref, t_ref, dl_ref,
                   cw_ref, sw_ref, cm_ref, sm_ref, kre_ref, kim_ref, act_ref, *, inv_scale):
    @pl.when((pl.program_id(0) == 0) & (pl.program_id(1) == 0))
    def _():
        fr = fr_ref[...]
        a = jnp.sin(fr * (_dot_f32(feats_ref[...], w1_ref[...]) + b1_ref[...]))
        act_ref[...] = jnp.sin(fr * (_dot_f32(a, w2_ref[...]) + b2_ref[...]))

    a = act_ref[...]
    window = jnp.exp(-t_ref[...] * jnp.abs(dl_ref[...]))
    h_f = _dot_f32(a, w3f_ref[...]) * window
    h_b = _dot_f32(a, w3b_ref[...]) * window
    row = lax.broadcasted_iota(jnp.int32, h_b.shape, 0)
    h_b = jnp.where(row == 0, 0.0, h_b)
    h = jnp.concatenate([h_f + h_b, h_f - h_b], axis=1).astype(BF16)
    tc = h_f.shape[1]
    for r in range(0, h.shape[0], HY_ROW_CHUNK):
        rows = slice(r, r + HY_ROW_CHUNK)
        c = _dot(cm_ref[rows, :], h)
        s = _dot(sm_ref[rows, :], h)
        cw = cw_ref[rows, :]
        sw = sw_ref[rows, :]
        kre_ref[rows, :] = (cw * c[:, :tc] + sw * s[:, :tc]) * inv_scale
        kim_ref[rows, :] = (sw * c[:, tc:] - cw * s[:, tc:]) * inv_scale


def _hyena_spectra(seq, hy_w1, hy_b1, hy_w2, hy_b2, hy_w3, hy_freq, cmat, smat):
    f32 = F32
    n = 2 * seq
    t = jnp.linspace(0.0, 1.0, seq, dtype=f32)[:, None]
    w = (2.0 * math.pi / seq) * jnp.arange(seq, dtype=f32)[:, None]
    bands = jnp.linspace(1e-4, HY_BANDS - 1, HY_BANDS, dtype=f32)[None, :]
    feats = jnp.concatenate([t, jnp.cos(bands * w), -jnp.sin(bands * w)], axis=-1)
    feats = jnp.pad(feats, ((0, 0), (0, LANES - HY_EMB)))
    pad_f = LANES - HY_FFN
    w1 = jnp.pad(hy_w1.astype(f32), ((0, LANES - HY_EMB), (0, pad_f)))
    b1 = jnp.pad(hy_b1.astype(f32), (0, pad_f))[None, :]
    w2 = jnp.pad(hy_w2.astype(f32), ((0, pad_f), (0, pad_f)))
    b2 = jnp.pad(hy_b2.astype(f32), (0, pad_f))[None, :]
    fr = jnp.pad(hy_freq.astype(f32), (0, pad_f))[None, :]
    w3 = jnp.pad(hy_w3.astype(f32), ((0, pad_f), (0, 0))).reshape(LANES, HY_ORDER, 2, HY_CH)
    w3f = w3[:, :, 0, :].reshape(LANES, HY_ORDER * HY_CH)
    w3b = w3[:, :, 1, :].reshape(LANES, HY_ORDER * HY_CH)
    max_decay = math.log(HY_TARGET) / HY_FAST_PCT
    min_decay = math.log(HY_TARGET) / HY_SLOW_PCT
    deltas = jnp.linspace(min_decay, max_decay, HY_CH, dtype=f32)[None, :]
    half_w = (math.pi / n) * (jnp.arange(seq, dtype=f32) + 0.5)
    cw = jnp.cos(half_w)[:, None]
    sw = jnp.sin(half_w)[:, None]

    tc = HY_CH_TILE
    ncb = HY_CH // tc
    small = lambda shape: pl.BlockSpec(shape, lambda o, c: (0,) * len(shape))
    out = pl.pallas_call(
        functools.partial(_filter_kernel, inv_scale=2.0 / n),
        grid=(HY_ORDER, ncb),
        in_specs=[
            small((seq, LANES)), small((LANES, LANES)), small((1, LANES)), small((LANES, LANES)),
            small((1, LANES)), small((1, LANES)),
            pl.BlockSpec((LANES, tc), lambda o, c: (0, o * ncb + c)),
            pl.BlockSpec((LANES, tc), lambda o, c: (0, o * ncb + c)),
            small((seq, 1)),
            pl.BlockSpec((1, tc), lambda o, c: (0, c)),
            small((seq, 1)), small((seq, 1)),
            _const_spec((seq, seq)), _const_spec((seq, seq)),
        ],
        out_specs=[pl.BlockSpec((None, seq, tc), lambda o, c: (o, 0, c))] * 2,
        out_shape=[jax.ShapeDtypeStruct((HY_ORDER, seq, HY_CH), f32)] * 2,
        scratch_shapes=[pltpu.VMEM((seq, LANES), f32)],
        compiler_params=_params("arbitrary", "arbitrary"),
        name="hyena_filter_spectra",
    )(feats, w1, b1, w2, b2, fr, w3f, w3b, t, deltas, cw, sw, cmat, smat)
    return out


def _dft_matrices(seq):
    n4 = 8 * seq
    blk = 64
    odd_t = 2 * jnp.arange(seq, dtype=jnp.int32) + 1

    def table(rows):
        ang = ((rows[:, None] * odd_t[None, :]) % n4).astype(F32) * (2.0 * math.pi / n4)
        return jnp.cos(ang), jnp.sin(ang)

    ca, sa = (a[:, None, :] for a in table(2 * blk * jnp.arange(seq // blk, dtype=jnp.int32)))
    cb, sb = (a[None, :, :] for a in table(2 * jnp.arange(blk, dtype=jnp.int32) + 1))
    cmat = (ca * cb - sa * sb).reshape(seq, seq)
    smat = (sa * cb + ca * sb).reshape(seq, seq)
    return cmat.astype(BF16), smat.astype(BF16)


def _half_dft_tables(seq, width):
    half = seq // 2
    odd_f = 2 * jnp.arange(half, dtype=jnp.int32) + 1
    u = jnp.arange(half, dtype=jnp.int32)
    ang = ((odd_f[:, None] * u[None, :]) % (2 * seq)).astype(F32) * (2.0 * math.pi / (2 * seq))
    ce, se = jnp.cos(ang), jnp.sin(ang)
    tw = []
    for p in range(2):
        ang_p = ((odd_f * (2 * p + 1)) % (8 * seq)).astype(F32) * (2.0 * math.pi / (8 * seq))
        tw += [jnp.cos(ang_p), jnp.sin(ang_p)]
    tw = jnp.broadcast_to(jnp.stack(tw)[:, :, None], (4, half, width))
    return ce.astype(BF16), se.astype(BF16), ce.T.astype(BF16), se.T.astype(BF16), tw


def _hyena_kernel(v_ref, x1_ref, x2_ref, wv_ref, wx1_ref, wx2_ref, bv_ref, bx1_ref, bx2_ref,
                  k1r_ref, k1i_ref, k2r_ref, k2i_ref, skip_ref, tw_ref, ce_ref, se_ref, cet_ref, set_ref,
                  o_ref, stage_ref, z_ref, pre_ref, pim_ref):
    seq = v_ref.shape[0]
    half = seq // 2
    row = lax.broadcasted_iota(jnp.int32, (half, v_ref.shape[1]), 0)
    first = row == 0
    last = row == half - 1
    parities = (pl.ds(0, half, stride=2), pl.ds(1, half, stride=2))
    lane_blocks = [slice(l, l + LANES) for l in range(0, v_ref.shape[1], LANES)]

    def deinterleave(u):
        for j, lanes in enumerate(lane_blocks):
            stage_ref[j] = u[:, lanes]
        return tuple(jnp.concatenate([stage_ref[j, par, :] for j in range(len(lane_blocks))], axis=1)
                     for par in parities)

    def short_conv(u_ref, w_ref, b_ref):
        ev, od = deinterleave(u_ref[...].astype(F32))
        w = w_ref[...]
        b = b_ref[...]
        od_prev = jnp.where(first, 0.0, pltpu.roll(od, 1, 0))
        ev_next = jnp.where(last, 0.0, pltpu.roll(ev, half - 1, 0))
        return (od_prev * w[0:1] + ev * w[1:2] + od * w[2:3] + b,
                ev * w[0:1] + od * w[1:2] + ev_next * w[2:3] + b)

    chunks = [slice(r, r + HY_ROW_CHUNK) for r in range(0, half, HY_ROW_CHUNK)]
    z0 = short_conv(v_ref, wv_ref, bv_ref)
    z_ref[0] = z0[0]
    z_ref[1] = z0[1]
    gate_refs = ((x1_ref, wx1_ref, bx1_ref), (x2_ref, wx2_ref, bx2_ref))
    skip = skip_ref[...]
    for o in range(HY_ORDER):
        zb = (z_ref[0].astype(BF16), z_ref[1].astype(BF16))
        for rows in chunks:
            ce = ce_ref[rows, :]
            se = se_ref[rows, :]
            r_, i_ = [], []
            for p in range(2):
                a = _dot(ce, zb[p])
                b = _dot(se, zb[p])
                c, s = tw_ref[2 * p, rows, :], tw_ref[2 * p + 1, rows, :]
                r_.append(c * a - s * b)
                i_.append(c * b + s * a)
            z1r, z1i = r_[0] + r_[1], -(i_[0] + i_[1])
            z2r, z2i = i_[0] - i_[1], r_[1] - r_[0]
            k1r, k1i = k1r_ref[o, rows, :], k1i_ref[o, rows, :]
            k2r, k2i = k2r_ref[o, rows, :], k2i_ref[o, rows, :]
            y1r, y1i = z1r * k1r - z1i * k1i, z1r * k1i + z1i * k1r
            y2r, y2i = z2r * k2r - z2i * k2i, z2r * k2i + z2i * k2r
            q = ((y1r - y2i, y1i - y2r), (y1r + y2i, y1i + y2r))
            for p in range(2):
                c, s = tw_ref[2 * p, rows, :], tw_ref[2 * p + 1, rows, :]
                pre_ref[p, rows, :] = (c * q[p][0] - s * q[p][1]).astype(BF16)
                pim_ref[p, rows, :] = (-(c * q[p][1] + s * q[p][0])).astype(BF16)
        gate = short_conv(*gate_refs[o])
        for p in range(2):
            pre = pre_ref[p]
            pim = pim_ref[p]
            for rows in chunks:
                y = _dot(cet_ref[rows, :], pre) + _dot(set_ref[rows, :], pim)
                z_ref[p, rows, :] = gate[p][rows] * (y + skip[o:o + 1] * z_ref[p, rows, :])
    for j, lanes in enumerate(lane_blocks):
        stage_ref[j, parities[0], :] = z_ref[0, :, lanes]
        stage_ref[j, parities[1], :] = z_ref[1, :, lanes]
        o_ref[:, lanes] = stage_ref[j].astype(o_ref.dtype)


def _hyena(hy, conv_w, conv_b, kre, kim, skip, batch, seq):
    tc = HY_CH_TILE
    ncb = HY_CH // tc
    half = seq // 2
    hy3 = hy.reshape(batch, seq, 3 * HY_CH)
    cb = conv_b[None, :]
    ce, se, cet, set_, tw = _half_dft_tables(seq, tc)
    k1r, k1i = kre[:, :half], kim[:, :half]
    k2r, k2i = kre[:, :half - 1:-1], kim[:, :half - 1:-1]
    u_spec = lambda g: pl.BlockSpec((None, seq, tc), lambda c, b: (b, 0, g * ncb + c))
    w_spec = lambda g: pl.BlockSpec((3, tc), lambda c, b: (0, g * ncb + c))
    b_spec = lambda g: pl.BlockSpec((1, tc), lambda c, b: (0, g * ncb + c))
    k_spec = pl.BlockSpec((HY_ORDER, half, tc), lambda c, b: (0, 0, c), pipeline_mode=pl.Buffered(1))
    return pl.pallas_call(
        _hyena_kernel,
        grid=(ncb, batch),
        in_specs=[u_spec(0), u_spec(1), u_spec(2), w_spec(0), w_spec(1), w_spec(2),
                  b_spec(0), b_spec(1), b_spec(2), k_spec, k_spec, k_spec, k_spec,
                  pl.BlockSpec((HY_ORDER, tc), lambda c, b: (0, c)),
                  _const_spec((4, half, tc)),
                  _const_spec((half, half)), _const_spec((half, half)),
                  _const_spec((half, half)), _const_spec((half, half))],
        out_specs=pl.BlockSpec((None, seq, tc), lambda c, b: (b, 0, c)),
        out_shape=jax.ShapeDtypeStruct((batch, seq, HY_CH), BF16),
        scratch_shapes=[pltpu.VMEM((tc // LANES, seq, LANES), F32), pltpu.VMEM((2, half, tc), F32),
                        pltpu.VMEM((2, half, tc), BF16), pltpu.VMEM((2, half, tc), BF16)],
        compiler_params=_params("arbitrary", "arbitrary"),
        name="hyena_long_conv",
    )(hy3, hy3, hy3, conv_w, conv_w, conv_w, cb, cb, cb, k1r, k1i, k2r, k2i, skip, tw, ce, se, cet, set_)


def _even_in_kernel(x_ref, g_ref, win_ref, qg_ref, wuq_ref, kvg_ref, wuk_ref, wuv_ref, ones_ref,
                    c_ref, slo_ref, shi_ref, hy_ref, q_ref, k_ref, v_ref):
    c0 = 3 * HY_CH
    c1 = c0 + MLA_Q_RANK
    c2 = c1 + MLA_KV_RANK
    half = MLA_ROPE // 2
    scale = MLA_QK ** -0.5 * LOG2E
    for r in range(0, x_ref.shape[0], ROW_SUBTILE):
        rows = slice(r, r + ROW_SUBTILE)
        xn = _rms(x_ref[rows, :], g_ref[...], EPS).astype(BF16)
        proj = _dot(xn, win_ref[...])
        hy_ref[rows, :] = proj[:, :c0].astype(hy_ref.dtype)
        c = c_ref[rows, :]
        s_lo = slo_ref[rows, :]
        s_hi = shi_ref[rows, :]
        qn = _rms(proj[:, c0:c1], qg_ref[...], EPS).astype(BF16)
        q = _dot(qn, wuq_ref[...])
        kvn = _rms(proj[:, c1:c2], kvg_ref[...], EPS).astype(BF16)
        kn = _dot(kvn, wuk_ref[...])
        v_ref[rows, :] = (_dot(kvn, wuv_ref[...]) + ones_ref[...]).astype(BF16)
        k_pe = _rope(pltpu.roll(proj[:, c2:c2 + LANES], MLA_NOPE, 1), c, s_lo, s_hi, half)
        for h in range(MLA_HEADS):
            blk = slice(h * LANES, (h + 1) * LANES)
            q_ref[rows, blk] = (_rope(q[:, blk], c, s_lo, s_hi, half) * scale).astype(BF16)
            k_ref[rows, blk] = (kn[:, blk] + k_pe).astype(BF16)


def _mla_rope_tables(seq):
    inv = ROPE_THETA ** (-jnp.arange(0, MLA_ROPE, 2, dtype=F32) / MLA_ROPE)
    ang = jnp.arange(seq, dtype=F32)[:, None] * inv[None, :]
    cos, sin = jnp.cos(ang), jnp.sin(ang)
    half = MLA_ROPE // 2
    one = jnp.ones((seq, MLA_NOPE), F32)
    zero = jnp.zeros((seq, MLA_NOPE), F32)
    tail1 = jnp.ones((seq, LANES - MLA_QK), F32)
    tail0 = jnp.zeros((seq, LANES - MLA_QK), F32)
    zh = jnp.zeros((seq, half), F32)
    c = jnp.concatenate([one, cos, cos, tail1], axis=1)
    s_lo = jnp.concatenate([zero, -sin, zh, tail0], axis=1)
    s_hi = jnp.concatenate([zero, zh, sin, tail0], axis=1)
    return c, s_lo, s_hi


def _pad_heads(w, heads, width):
    rows = w.shape[0]
    w = w.reshape(rows, heads, width)
    return jnp.pad(w, ((0, 0), (0, 0), (0, LANES - width))).reshape(rows, heads * LANES)


def _even_in(x2d, g, w_in, q_norm, w_uq, kv_norm, w_ukv, seq):
    rows = x2d.shape[0]
    tm = ROW_TILE
    n_in = w_in.shape[1]
    n_pad = -(-(n_in + LANES - MLA_ROPE) // LANES) * LANES
    win = jnp.pad(w_in, ((0, 0), (0, n_pad - n_in))).astype(BF16)
    wuq = _pad_heads(w_uq, MLA_HEADS, MLA_QK).astype(BF16)
    wkv = w_ukv.reshape(MLA_KV_RANK, MLA_HEADS, MLA_NOPE + MLA_V)
    wuk = _pad_heads(wkv[:, :, :MLA_NOPE].reshape(MLA_KV_RANK, -1), MLA_HEADS, MLA_NOPE).astype(BF16)
    wuv = _pad_heads(wkv[:, :, MLA_NOPE:].reshape(MLA_KV_RANK, -1), MLA_HEADS, MLA_V).astype(BF16)
    c, s_lo, s_hi = _mla_rope_tables(seq)
    tiles_per_seq = seq // tm
    row_spec = lambda width: pl.BlockSpec((tm, width), lambda i: (i, 0))
    tab_spec = pl.BlockSpec((tm, LANES), lambda i: (i % tiles_per_seq, 0))
    hw = MLA_HEADS * LANES
    ones_col = jnp.tile((jnp.arange(LANES) == MLA_V).astype(F32), MLA_HEADS)[None, :]
    return pl.pallas_call(
        _even_in_kernel,
        grid=(rows // tm,),
        in_specs=[row_spec(D_MODEL), _const_spec((1, D_MODEL)), _const_spec(win.shape),
                  _const_spec((1, MLA_Q_RANK)), _const_spec(wuq.shape),
                  _const_spec((1, MLA_KV_RANK)), _const_spec(wuk.shape), _const_spec(wuv.shape),
                  _const_spec((1, hw)), tab_spec, tab_spec, tab_spec],
        out_specs=[row_spec(3 * HY_CH), row_spec(hw), row_spec(hw), row_spec(hw)],
        out_shape=[jax.ShapeDtypeStruct((rows, 3 * HY_CH), BF16)] + [jax.ShapeDtypeStruct((rows, hw), BF16)] * 3,
        compiler_params=_params("arbitrary"),
        name="even_in_proj",
    )(x2d, g[None, :], win, q_norm[None, :], wuq, kv_norm[None, :], wuk, wuv, ones_col, c, s_lo, s_hi)


def _attn_kernel(q_ref, k_ref, v_ref, o_ref, *, heads, sum_lane):
    for h in range(heads):
        blk = slice(h * LANES, (h + 1) * LANES)
        s = _dot_nt(q_ref[:, blk], k_ref[:, blk])
        e = jnp.exp2(s - jnp.max(s, axis=-1, keepdims=True)).astype(BF16)
        o = _dot(e, v_ref[:, blk])
        l = o[:, sum_lane:sum_lane + 1]
        o_ref[:, blk] = (o / l).astype(o_ref.dtype)


def _attention(q, k, v, batch, seq, heads, sum_lane):
    tq = ATTN_Q_TILE
    hw = heads * LANES
    q3, k3, v3 = (a.reshape(batch, seq, hw) for a in (q, k, v))
    q_spec = pl.BlockSpec((None, tq, hw), lambda b, i: (b, i, 0))
    kv_spec = pl.BlockSpec((None, seq, hw), lambda b, i: (b, 0, 0))
    out = pl.pallas_call(
        functools.partial(_attn_kernel, heads=heads, sum_lane=sum_lane),
        grid=(batch, seq // tq),
        in_specs=[q_spec, kv_spec, kv_spec],
        out_specs=q_spec,
        out_shape=jax.ShapeDtypeStruct((batch, seq, hw), BF16),
        compiler_params=_params("arbitrary", "arbitrary"),
        name="mla_attention",
    )(q3, k3, v3)
    return out.reshape(batch * seq, hw)


def _odd_in_kernel(x_ref, g_ref, w_ref, c_ref, slo_ref, shi_ref, q_ref, k_ref, v_ref):
    half = DIFF_HD // 2
    scale = DIFF_HD ** -0.5 * LOG2E
    for r in range(0, x_ref.shape[0], ROW_SUBTILE):
        rows = slice(r, r + ROW_SUBTILE)
        xn = _rms(x_ref[rows, :], g_ref[...], EPS).astype(BF16)
        qkv = _dot(xn, w_ref[...])
        c = c_ref[rows, :]
        s_lo = slo_ref[rows, :]
        s_hi = shi_ref[rows, :]
        for h in range(DIFF_HEADS):
            blk = slice(h * LANES, (h + 1) * LANES)
            kblk = slice(D_MODEL + h * LANES, D_MODEL + (h + 1) * LANES)
            q_ref[rows, blk] = (_rope(qkv[:, blk], c, s_lo, s_hi, half) * scale).astype(BF16)
            k_ref[rows, blk] = _rope(qkv[:, kblk], c, s_lo, s_hi, half).astype(BF16)
        v_ref[rows, :] = qkv[:, 2 * D_MODEL:].astype(BF16)


def _diff_rope_tables(seq):
    inv = ROPE_THETA ** (-jnp.arange(0, DIFF_HD, 2, dtype=F32) / DIFF_HD)
    ang = jnp.arange(seq, dtype=F32)[:, None] * inv[None, :]
    cos, sin = jnp.cos(ang), jnp.sin(ang)
    zero = jnp.zeros_like(sin)
    reps = LANES // DIFF_HD
    c = jnp.tile(jnp.concatenate([cos, cos], axis=1), (1, reps))
    s_lo = jnp.tile(jnp.concatenate([-sin, zero], axis=1), (1, reps))
    s_hi = jnp.tile(jnp.concatenate([zero, sin], axis=1), (1, reps))
    return c, s_lo, s_hi


def _odd_in(x2d, g, w_qkv, seq):
    rows = x2d.shape[0]
    tm = ROW_TILE
    c, s_lo, s_hi = _diff_rope_tables(seq)
    tiles_per_seq = seq // tm
    row_spec = lambda width: pl.BlockSpec((tm, width), lambda i: (i, 0))
    tab_spec = pl.BlockSpec((tm, LANES), lambda i: (i % tiles_per_seq, 0))
    w = w_qkv.astype(BF16)
    return pl.pallas_call(
        _odd_in_kernel,
        grid=(rows // tm,),
        in_specs=[row_spec(D_MODEL), _const_spec((1, D_MODEL)), _const_spec(w.shape),
                  tab_spec, tab_spec, tab_spec],
        out_specs=[row_spec(D_MODEL)] * 3,
        out_shape=[jax.ShapeDtypeStruct((rows, D_MODEL), BF16)] * 3,
        compiler_params=_params("arbitrary"),
        name="odd_in_proj",
    )(x2d, g[None, :], w, c, s_lo, s_hi)


def _diff_attn_kernel(lam_ref, q_ref, k_ref, v_ref, g_ref, o_ref, s_ref, e_ref, *, lam_init):
    lp = lam_ref[...]
    lam = (jnp.exp(jnp.sum(lp[0:1] * lp[1:2], axis=-1, keepdims=True))
           - jnp.exp(jnp.sum(lp[2:3] * lp[3:4], axis=-1, keepdims=True)) + lam_init)
    g = g_ref[...] * (1.0 - lam_init)
    tq = q_ref.shape[0]
    first_head = lax.broadcasted_iota(jnp.int32, (tq, LANES), 1) < DIFF_HD
    blocks = [slice(r, r + SOFTMAX_ROWS) for r in range(0, tq, SOFTMAX_ROWS)]

    def scores(h):
        blk = slice(h * LANES, (h + 1) * LANES)
        q = q_ref[:, blk]
        k = k_ref[:, blk]
        zero = jnp.zeros_like(q)
        s_ref[h % 2, 0] = _dot_nt(jnp.where(first_head, q, zero), k)
        s_ref[h % 2, 1] = _dot_nt(jnp.where(first_head, zero, q), k)

    def numerator(slot, j):
        m = jnp.max(s_ref[slot, j], axis=-1, keepdims=True)
        sums = []
        for rows in blocks:
            m_rows = m[rows]
            acc = None
            for c0 in range(0, s_ref.shape[-1], SOFTMAX_KEYS):
                keys = slice(c0, c0 + SOFTMAX_KEYS)
                e = jnp.exp2(s_ref[slot, j, rows, keys] - m_rows)
                for l0 in range(0, SOFTMAX_KEYS, LANES):
                    part = e[:, l0:l0 + LANES]
                    acc = part if acc is None else acc + part
                e_ref[j, rows, keys] = e.astype(BF16)
            sums.append(jnp.sum(acc, axis=-1, keepdims=True))
        return jnp.concatenate(sums, axis=0)

    scores(0)
    for h in range(DIFF_HEADS):
        if h + 1 < DIFF_HEADS:
            scores(h + 1)
        blk = slice(h * LANES, (h + 1) * LANES)
        l1 = numerator(h % 2, 0)
        l2 = numerator(h % 2, 1)
        c = (lam * l1 / l2).astype(BF16)
        for rows in blocks:
            e_ref[0, rows, :] = e_ref[0, rows, :] - c[rows] * e_ref[1, rows, :]
        o = _dot(e_ref[0], v_ref[:, blk]) / l1
        o_ref[:, blk] = (o * lax.rsqrt(jnp.mean(o * o, axis=-1, keepdims=True) + SUBLN_EPS) * g).astype(o_ref.dtype)


def _diff_attention(q, k, v, lam_params, subln, lam_init, batch, seq):
    tq = ATTN_Q_TILE
    q3, k3, v3 = (a.reshape(batch, seq, D_MODEL) for a in (q, k, v))
    q_spec = pl.BlockSpec((None, tq, D_MODEL), lambda b, i: (b, i, 0))
    kv_spec = pl.BlockSpec((None, seq, D_MODEL), lambda b, i: (b, 0, 0))
    lam_tile = jnp.pad(jnp.stack(lam_params).astype(F32), ((0, 4), (0, LANES - DIFF_HD)))
    out = pl.pallas_call(
        functools.partial(_diff_attn_kernel, lam_init=lam_init),
        grid=(batch, seq // tq),
        in_specs=[pl.BlockSpec((8, LANES), lambda b, i: (0, 0)), q_spec, kv_spec, kv_spec,
                  pl.BlockSpec((1, LANES), lambda b, i: (0, 0))],
        out_specs=q_spec,
        out_shape=jax.ShapeDtypeStruct((batch, seq, D_MODEL), BF16),
        scratch_shapes=[pltpu.VMEM((2, 2, tq, seq), F32), pltpu.VMEM((2, tq, seq), BF16)],
        compiler_params=_params("arbitrary", "arbitrary"),
        name="diff_attention",
    )(lam_tile, q3, k3, v3, subln[None, :])
    return out.reshape(batch * seq, D_MODEL)


def _mem_kv_kernel(mem_ref, g_ref, wkv_ref, kt_ref, v_ref):
    mn = _rms(mem_ref[...], g_ref[...], EPS).astype(BF16)
    for layer in range(wkv_ref.shape[0]):
        kv = _dot(mn, wkv_ref[layer])
        kt_ref[layer] = kv[:, :D_MODEL].T.astype(BF16)
        v_ref[layer] = kv[:, D_MODEL:].astype(BF16)


def _mem_kv(mem, mem_norm, xa_wkv):
    batch, n_mem, _ = mem.shape
    depth = xa_wkv.shape[0]
    w = xa_wkv.astype(BF16)
    return pl.pallas_call(
        _mem_kv_kernel,
        grid=(batch,),
        in_specs=[pl.BlockSpec((None, n_mem, D_MODEL), lambda b: (b, 0, 0)), _const_spec((1, D_MODEL)),
                  _const_spec(w.shape)],
        out_specs=[pl.BlockSpec((depth, None, D_MODEL, n_mem), lambda b: (0, b, 0, 0)),
                   pl.BlockSpec((depth, None, n_mem, D_MODEL), lambda b: (0, b, 0, 0))],
        out_shape=[jax.ShapeDtypeStruct((depth, batch, D_MODEL, n_mem), BF16),
                   jax.ShapeDtypeStruct((depth, batch, n_mem, D_MODEL), BF16)],
        compiler_params=_params("arbitrary"),
        name="memory_kv",
    )(mem, mem_norm[None, :], w)


def _cross_mlp_kernel(*refs, n_mix, final_norm):
    x_ref, o_ref = refs[0], refs[-1]
    mix = refs[1:1 + 2 * n_mix]
    gc_ref, wq_ref, kt_ref, v_ref, wo_ref, gm_ref, wup_ref, wdn_ref, gf_ref = refs[1 + 2 * n_mix:-1]
    x = x_ref[...]
    for a_ref, w_ref in zip(mix[0::2], mix[1::2]):
        x = x + _dot(a_ref[...], w_ref[...])
    hn = _rms(x, gc_ref[...], EPS).astype(BF16)
    q = (_dot(hn, wq_ref[...]) * (X_HD ** -0.5 * LOG2E)).astype(BF16)
    heads = []
    for h in range(X_HEADS):
        blk = slice(h * X_HD, (h + 1) * X_HD)
        s = _dot(q[:, blk], kt_ref[blk, :])
        e = jnp.exp2(s - jnp.max(s, axis=-1, keepdims=True))
        l = jnp.sum(e, axis=-1, keepdims=True)
        heads.append((_dot(e.astype(BF16), v_ref[:, blk]) / l).astype(BF16))
    x = x + _dot(jnp.concatenate(heads, axis=1), wo_ref[...])
    hn = _rms(x, gm_ref[...], EPS).astype(BF16)
    u = jnp.maximum(_dot(hn, wup_ref[...]), 0.0)
    x = x + _dot((u * u).astype(BF16), wdn_ref[...])
    if final_norm:
        x = _rms(x, gf_ref[...], EPS)
    o_ref[...] = x


def _mix_cross_mlp(x2d, mix_pairs, g_cross, wq, kt, v, wo, g_mlp, w_up, w_down, g_final, seq, final_norm):
    rows = x2d.shape[0]
    tm = MLP_ROW_TILE
    tiles_per_seq = seq // tm
    n_mem = v.shape[1]
    row_spec = lambda width: pl.BlockSpec((tm, width), lambda i: (i, 0))
    vec_spec = _const_spec((1, D_MODEL))
    mix_specs, mix_args = [], []
    for a, w in mix_pairs:
        mix_specs += [row_spec(a.shape[1]), _const_spec(w.shape)]
        mix_args += [a, w]
    return pl.pallas_call(
        functools.partial(_cross_mlp_kernel, n_mix=len(mix_pairs), final_norm=final_norm),
        grid=(rows // tm,),
        in_specs=[row_spec(D_MODEL)] + mix_specs + [
            vec_spec, _const_spec((D_MODEL, D_MODEL)),
            pl.BlockSpec((None, D_MODEL, n_mem), lambda i: (i // tiles_per_seq, 0, 0)),
            pl.BlockSpec((None, n_mem, D_MODEL), lambda i: (i // tiles_per_seq, 0, 0)),
            _const_spec((D_MODEL, D_MODEL)), vec_spec, _const_spec((D_MODEL, D_FF)),
            _const_spec((D_FF, D_MODEL)), vec_spec],
        out_specs=row_spec(D_MODEL),
        out_shape=jax.ShapeDtypeStruct((rows, D_MODEL), F32),
        compiler_params=_params("arbitrary"),
        name="mix_cross_mlp",
    )(x2d, *mix_args, g_cross[None, :], wq.astype(BF16), kt, v, wo.astype(BF16), g_mlp[None, :],
      w_up.astype(BF16), w_down.astype(BF16), g_final[None, :])


def kernel(x, mem, ev_w_in, ev_conv_w, ev_conv_b, hy_w1, hy_b1, hy_w2, hy_b2, hy_w3, hy_freq, hy_skip,
           mla_q_norm, mla_w_uq, mla_kv_norm, mla_w_ukv, ev_w_out, od_w_qkv, dif_lq1, dif_lk1, dif_lq2,
           dif_lk2, dif_subln, od_w_out, norm_mix, norm_cross, norm_mlp, xa_wq, xa_wkv, xa_wo, mlp_up,
           mlp_down, mem_norm, final_norm):
    batch, seq, d = x.shape
    depth = norm_mix.shape[0]
    assert d == D_MODEL and seq % ROW_TILE == 0 and seq % MLP_ROW_TILE == 0 and seq % ATTN_Q_TILE == 0
    x2d = x.reshape(batch * seq, d)
    kt_all, v_all = _mem_kv(mem, mem_norm, xa_wkv)
    cmat, smat = _dft_matrices(seq)
    for i in range(depth):
        j = i // 2
        if i % 2 == 0:
            kre, kim = _hyena_spectra(seq, hy_w1[j], hy_b1[j], hy_w2[j], hy_b2[j], hy_w3[j], hy_freq[j],
                                      cmat, smat)
            hy, q, k, v = _even_in(x2d, norm_mix[i], ev_w_in[j], mla_q_norm[j], mla_w_uq[j],
                                   mla_kv_norm[j], mla_w_ukv[j], seq)
            z = _hyena(hy, ev_conv_w[j], ev_conv_b[j], kre, kim, hy_skip[j], batch, seq)
            o = _attention(q, k, v, batch, seq, MLA_HEADS, sum_lane=MLA_V)
            w_out = ev_w_out[j]
            w_mla = w_out[HY_CH:].reshape(MLA_HEADS, MLA_V, d)
            w_mla = jnp.pad(w_mla, ((0, 0), (0, LANES - MLA_V), (0, 0))).reshape(MLA_HEADS * LANES, d)
            mix = [(z.reshape(batch * seq, HY_CH), w_out[:HY_CH].astype(BF16)), (o, w_mla.astype(BF16))]
        else:
            lam_init = 0.8 - 0.6 * math.exp(-0.3 * i)
            q, k, v = _odd_in(x2d, norm_mix[i], od_w_qkv[j], seq)
            o = _diff_attention(q, k, v, (dif_lq1[j], dif_lk1[j], dif_lq2[j], dif_lk2[j]), dif_subln[j],
                                lam_init, batch, seq)
            mix = [(o, od_w_out[j].astype(BF16))]
        x2d = _mix_cross_mlp(x2d, mix, norm_cross[i], xa_wq[i], kt_all[i], v_all[i], xa_wo[i], norm_mlp[i],
                             mlp_up[i], mlp_down[i], final_norm, seq, final_norm=(i == depth - 1))
    return x2d.reshape(batch, seq, d)
```

```python
import functools
import math

import jax
import jax.numpy as jnp
import numpy as np
from jax import lax
from jax.experimental import pallas as pl
from jax.experimental.pallas import tpu as pltpu

F32 = jnp.float32
BF16 = jnp.bfloat16

D_MODEL = 1024
EPS = 1e-6
ROPE_THETA = 10000.0
HY_CH = 512
HY_ORDER = 2
HY_EMB = 33
HY_BANDS = (HY_EMB - 1) // 2
HY_FFN = 64
HY_FAST_PCT = 0.3
HY_SLOW_PCT = 1.5
HY_TARGET = 1e-2
MLA_HEADS = 8
MLA_NOPE = 64
MLA_ROPE = 32
MLA_V = 64
MLA_QK = MLA_NOPE + MLA_ROPE
MLA_Q_RANK = D_MODEL // 4
MLA_KV_RANK = D_MODEL // 8
DIFF_HEADS = 8
DIFF_HD = D_MODEL // DIFF_HEADS // 2
SUBLN_EPS = 1e-5
X_HEADS = 4
X_HD = D_MODEL // X_HEADS
D_FF = 4 * D_MODEL

LOG2E = math.log2(math.e)
LANES = 128
V7X_VMEM_BYTES = 64 * 1024 * 1024
VMEM_LIMIT = V7X_VMEM_BYTES - 8 * 1024 * 1024

ROW_TILE = 1024
MLP_ROW_TILE = 512
ROW_SUBTILE = 256
ATTN_Q_TILE = 512
SOFTMAX_ROWS = 16
SOFTMAX_KEYS = 512
HY_CH_TILE = 256
HY_ROW_CHUNK = 512


def _const_spec(shape):
    nd = len(shape)
    return pl.BlockSpec(shape, lambda *_: (0,) * nd, pipeline_mode=pl.Buffered(1))


def _params(*sem):
    return pltpu.CompilerParams(dimension_semantics=sem, vmem_limit_bytes=VMEM_LIMIT)


def _rms(x, g, eps):
    return x * lax.rsqrt(jnp.mean(x * x, axis=-1, keepdims=True) + eps) * g


def _rope(x, c, s_lo, s_hi, half):
    return x * c + pltpu.roll(x, LANES - half, 1) * s_lo + pltpu.roll(x, half, 1) * s_hi


def _dot(a, b):
    return jnp.dot(a, b, preferred_element_type=F32)


def _dot_nt(a, b):
    return lax.dot_general(a, b, (((1,), (1,)), ((), ())), preferred_element_type=F32)


def _dot_f32(a, b):
    return jnp.dot(a, b, preferred_element_type=F32, precision=lax.Precision.HIGHEST)


def _filter_kernel(feats_ref, w1_ref, b1_ref, w2_ref, b2_ref, fr_ref, w3f_ref, w3b_ref, t_ref, dl_ref, sgn_ref,
                   cw_ref, sw_ref, cm_ref, sm_ref, k1r_ref, k1i_ref, k2r_ref, k2i_ref, act_ref, *, inv_scale):
    @pl.when((pl.program_id(0) == 0) & (pl.program_id(1) == 0))
    def _():
        fr = fr_ref[...]
        a = jnp.sin(fr * (_dot_f32(feats_ref[...], w1_ref[...]) + b1_ref[...]))
        act_ref[...] = jnp.sin(fr * (_dot_f32(a, w2_ref[...]) + b2_ref[...]))

    a = act_ref[...]
    window = jnp.exp(-t_ref[...] * jnp.abs(dl_ref[...]))
    h_f = _dot_f32(a, w3f_ref[...]) * window
    h_b = _dot_f32(a, w3b_ref[...]) * window
    row = lax.broadcasted_iota(jnp.int32, h_b.shape, 0)
    h_b = jnp.where(row == 0, 0.0, h_b)
    h = jnp.concatenate([h_f + h_b, h_f - h_b], axis=1)
    h = jnp.concatenate([h, h * sgn_ref[...]], axis=1).astype(BF16)
    tc = h_f.shape[1]
    col = lambda x, j: x[:, j * tc:(j + 1) * tc]
    for r in range(0, cm_ref.shape[0], HY_ROW_CHUNK):
        rows = slice(r, r + HY_ROW_CHUNK)
        c = _dot(cm_ref[rows, :], h)
        s = _dot(sm_ref[rows, :], h)
        cw, sw = cw_ref[0, rows, :], sw_ref[0, rows, :]
        k1r_ref[rows, :] = (cw * col(c, 0) + sw * col(s, 0)) * inv_scale
        k1i_ref[rows, :] = (sw * col(c, 1) - cw * col(s, 1)) * inv_scale
        cw, sw = cw_ref[1, rows, :], sw_ref[1, rows, :]
        k2r_ref[rows, :] = (cw * col(s, 2) + sw * col(c, 2)) * inv_scale
        k2i_ref[rows, :] = (sw * col(s, 3) - cw * col(c, 3)) * inv_scale


def _hyena_spectra(seq, hy_w1, hy_b1, hy_w2, hy_b2, hy_w3, hy_freq):
    f32 = F32
    n = 2 * seq
    half = seq // 2
    t = jnp.linspace(0.0, 1.0, seq, dtype=f32)[:, None]
    w = (2.0 * math.pi / seq) * jnp.arange(seq, dtype=f32)[:, None]
    bands = jnp.linspace(1e-4, HY_BANDS - 1, HY_BANDS, dtype=f32)[None, :]
    feats = jnp.concatenate([t, jnp.cos(bands * w), -jnp.sin(bands * w)], axis=-1)
    feats = jnp.pad(feats, ((0, 0), (0, LANES - HY_EMB)))
    pad_f = LANES - HY_FFN
    w1 = jnp.pad(hy_w1.astype(f32), ((0, LANES - HY_EMB), (0, pad_f)))
    b1 = jnp.pad(hy_b1.astype(f32), (0, pad_f))[None, :]
    w2 = jnp.pad(hy_w2.astype(f32), ((0, pad_f), (0, pad_f)))
    b2 = jnp.pad(hy_b2.astype(f32), (0, pad_f))[None, :]
    fr = jnp.pad(hy_freq.astype(f32), (0, pad_f))[None, :]
    w3 = jnp.pad(hy_w3.astype(f32), ((0, pad_f), (0, 0))).reshape(LANES, HY_ORDER, 2, HY_CH)
    w3f = w3[:, :, 0, :].reshape(LANES, HY_ORDER * HY_CH)
    w3b = w3[:, :, 1, :].reshape(LANES, HY_ORDER * HY_CH)
    max_decay = math.log(HY_TARGET) / HY_FAST_PCT
    min_decay = math.log(HY_TARGET) / HY_SLOW_PCT
    deltas = jnp.linspace(min_decay, max_decay, HY_CH, dtype=f32)[None, :]
    sgn = (1.0 - 2.0 * (jnp.arange(seq) % 2).astype(f32))[:, None]
    f_lo = jnp.arange(half, dtype=f32)
    half_w = (math.pi / n) * (jnp.stack([f_lo, seq - 1 - f_lo]) + 0.5)
    cw = jnp.cos(half_w)[:, :, None]
    sw = jnp.sin(half_w)[:, :, None]
    cmat, smat = _dft_matrices(seq)

    tc = HY_CH_TILE
    ncb = HY_CH // tc
    small = lambda shape: pl.BlockSpec(shape, lambda o, c: (0,) * len(shape))
    return pl.pallas_call(
        functools.partial(_filter_kernel, inv_scale=2.0 / n),
        grid=(HY_ORDER, ncb),
        in_specs=[
            small((seq, LANES)), small((LANES, LANES)), small((1, LANES)), small((LANES, LANES)),
            small((1, LANES)), small((1, LANES)),
            pl.BlockSpec((LANES, tc), lambda o, c: (0, o * ncb + c)),
            pl.BlockSpec((LANES, tc), lambda o, c: (0, o * ncb + c)),
            small((seq, 1)),
            pl.BlockSpec((1, tc), lambda o, c: (0, c)),
            small((seq, 1)), small((2, half, 1)), small((2, half, 1)),
            _const_spec((half, seq)), _const_spec((half, seq)),
        ],
        out_specs=[pl.BlockSpec((None, half, tc), lambda o, c: (o, 0, c))] * 4,
        out_shape=[jax.ShapeDtypeStruct((HY_ORDER, half, HY_CH), f32)] * 4,
        scratch_shapes=[pltpu.VMEM((seq, LANES), f32)],
        compiler_params=_params("arbitrary", "arbitrary"),
        name="hyena_filter_spectra",
    )(feats, w1, b1, w2, b2, fr, w3f, w3b, t, deltas, sgn, cw, sw, cmat, smat)


def _cos_sin_outer(row_mult, col_mult, modulus, blk=32):
    rows = row_mult.shape[0]
    step = row_mult[1] - row_mult[0]

    def table(mult):
        ang = ((mult[:, None] * col_mult[None, :]) % modulus).astype(F32) * (2.0 * math.pi / modulus)
        return jnp.cos(ang), jnp.sin(ang)

    ca, sa = (a[:, None, :] for a in table(blk * step * jnp.arange(rows // blk, dtype=jnp.int32)))
    cb, sb = (a[None, :, :] for a in table(row_mult[:blk]))
    shape = (rows, col_mult.shape[0])
    return (ca * cb - sa * sb).reshape(shape), (sa * cb + ca * sb).reshape(shape)


def _dft_matrices(seq):
    odd = 2 * jnp.arange(seq, dtype=jnp.int32) + 1
    cmat, smat = _cos_sin_outer(odd[:seq // 2], odd, 8 * seq)
    return cmat.astype(BF16), smat.astype(BF16)


def _half_dft_tables(seq, width):
    half = seq // 2
    odd_f = 2 * jnp.arange(half, dtype=jnp.int32) + 1
    ce, se = _cos_sin_outer(odd_f, jnp.arange(half, dtype=jnp.int32), 2 * seq)
    tw = []
    for p in range(2):
        ang_p = ((odd_f * (2 * p + 1)) % (8 * seq)).astype(F32) * (2.0 * math.pi / (8 * seq))
        tw += [jnp.cos(ang_p), jnp.sin(ang_p)]
    tw = jnp.broadcast_to(jnp.stack(tw)[:, :, None], (4, half, width))
    return ce.astype(BF16), se.astype(BF16), ce.T.astype(BF16), se.T.astype(BF16), tw


def _hyena_kernel(v_ref, x1_ref, x2_ref, wv_ref, wx1_ref, wx2_ref, bv_ref, bx1_ref, bx2_ref,
                  k1r_ref, k1i_ref, k2r_ref, k2i_ref, skip_ref, tw_ref, ce_ref, se_ref, cet_ref, set_ref,
                  o_ref, stage_ref, z_ref, pre_ref, pim_ref):
    seq = v_ref.shape[0]
    half = seq // 2
    row = lax.broadcasted_iota(jnp.int32, (half, v_ref.shape[1]), 0)
    first = row == 0
    last = row == half - 1
    parities = (pl.ds(0, half, stride=2), pl.ds(1, half, stride=2))
    lane_blocks = [slice(l, l + LANES) for l in range(0, v_ref.shape[1], LANES)]

    def deinterleave(u):
        for j, lanes in enumerate(lane_blocks):
            stage_ref[j] = u[:, lanes]
        return tuple(jnp.concatenate([stage_ref[j, par, :] for j in range(len(lane_blocks))], axis=1)
                     for par in parities)

    def short_conv(u_ref, w_ref, b_ref):
        ev, od = deinterleave(u_ref[...].astype(F32))
        w = w_ref[...]
        b = b_ref[...]
        od_prev = jnp.where(first, 0.0, pltpu.roll(od, 1, 0))
        ev_next = jnp.where(last, 0.0, pltpu.roll(ev, half - 1, 0))
        return (od_prev * w[0:1] + ev * w[1:2] + od * w[2:3] + b,
                ev * w[0:1] + od * w[1:2] + ev_next * w[2:3] + b)

    chunks = [slice(r, r + HY_ROW_CHUNK) for r in range(0, half, HY_ROW_CHUNK)]
    z0 = short_conv(v_ref, wv_ref, bv_ref)
    z_ref[0] = z0[0]
    z_ref[1] = z0[1]
    gate_refs = ((x1_ref, wx1_ref, bx1_ref), (x2_ref, wx2_ref, bx2_ref))
    skip = skip_ref[...]
    for o in range(HY_ORDER):
        zb = (z_ref[0].astype(BF16), z_ref[1].astype(BF16))
        for rows in chunks:
            ce = ce_ref[rows, :]
            se = se_ref[rows, :]
            r_, i_ = [], []
            for p in range(2):
                a = _dot(ce, zb[p])
                b = _dot(se, zb[p])
                c, s = tw_ref[2 * p, rows, :], tw_ref[2 * p + 1, rows, :]
                r_.append(c * a - s * b)
                i_.append(c * b + s * a)
            z1r, z1i = r_[0] + r_[1], -(i_[0] + i_[1])
            z2r, z2i = i_[0] - i_[1], r_[1] - r_[0]
            k1r, k1i = k1r_ref[o, rows, :], k1i_ref[o, rows, :]
            k2r, k2i = k2r_ref[o, rows, :], k2i_ref[o, rows, :]
            y1r, y1i = z1r * k1r - z1i * k1i, z1r * k1i + z1i * k1r
            y2r, y2i = z2r * k2r - z2i * k2i, z2r * k2i + z2i * k2r
            q = ((y1r - y2i, y1i - y2r), (y1r + y2i, y1i + y2r))
            for p in range(2):
                c, s = tw_ref[2 * p, rows, :], tw_ref[2 * p + 1, rows, :]
                pre_ref[p, rows, :] = (c * q[p][0] - s * q[p][1]).astype(BF16)
                pim_ref[p, rows, :] = (-(c * q[p][1] + s * q[p][0])).astype(BF16)
        gate = short_conv(*gate_refs[o])
        for p in range(2):
            pre = pre_ref[p]
            pim = pim_ref[p]
            for rows in chunks:
                y = _dot(cet_ref[rows, :], pre) + _dot(set_ref[rows, :], pim)
                z_ref[p, rows, :] = gate[p][rows] * (y + skip[o:o + 1] * z_ref[p, rows, :])
    for j, lanes in enumerate(lane_blocks):
        stage_ref[j, parities[0], :] = z_ref[0, :, lanes]
        stage_ref[j, parities[1], :] = z_ref[1, :, lanes]
        o_ref[:, lanes] = stage_ref[j].astype(o_ref.dtype)


def _hyena(hy, conv_w, conv_b, spectra, skip, batch, seq):
    tc = HY_CH_TILE
    ncb = HY_CH // tc
    half = seq // 2
    hy3 = hy.reshape(batch, seq, 3 * HY_CH)
    cb = conv_b[None, :]
    ce, se, cet, set_, tw = _half_dft_tables(seq, tc)
    k1r, k1i, k2r, k2i = spectra
    u_spec = lambda g: pl.BlockSpec((None, seq, tc), lambda c, b: (b, 0, g * ncb + c))
    w_spec = lambda g: pl.BlockSpec((3, tc), lambda c, b: (0, g * ncb + c))
    b_spec = lambda g: pl.BlockSpec((1, tc), lambda c, b: (0, g * ncb + c))
    k_spec = pl.BlockSpec((HY_ORDER, half, tc), lambda c, b: (0, 0, c), pipeline_mode=pl.Buffered(1))
    return pl.pallas_call(
        _hyena_kernel,
        grid=(ncb, batch),
        in_specs=[u_spec(0), u_spec(1), u_spec(2), w_spec(0), w_spec(1), w_spec(2),
                  b_spec(0), b_spec(1), b_spec(2), k_spec, k_spec, k_spec, k_spec,
                  pl.BlockSpec((HY_ORDER, tc), lambda c, b: (0, c)),
                  _const_spec((4, half, tc)),
                  _const_spec((half, half)), _const_spec((half, half)),
                  _const_spec((half, half)), _const_spec((half, half))],
        out_specs=pl.BlockSpec((None, seq, tc), lambda c, b: (b, 0, c)),
        out_shape=jax.ShapeDtypeStruct((batch, seq, HY_CH), BF16),
        scratch_shapes=[pltpu.VMEM((tc // LANES, seq, LANES), F32), pltpu.VMEM((2, half, tc), F32),
                        pltpu.VMEM((2, half, tc), BF16), pltpu.VMEM((2, half, tc), BF16)],
        compiler_params=_params("arbitrary", "arbitrary"),
        name="hyena_long_conv",
    )(hy3, hy3, hy3, conv_w, conv_w, conv_w, cb, cb, cb, k1r, k1i, k2r, k2i, skip, tw, ce, se, cet, set_)


def _even_in_kernel(x_ref, g_ref, win_ref, qg_ref, wuq_ref, kvg_ref, wuk_ref, wuv_ref, ones_ref,
                    c_ref, slo_ref, shi_ref, hy_ref, q_ref, k_ref, v_ref):
    c0 = 3 * HY_CH
    c1 = c0 + MLA_Q_RANK
    c2 = c1 + MLA_KV_RANK
    half = MLA_ROPE // 2
    scale = MLA_QK ** -0.5 * LOG2E
    for r in range(0, x_ref.shape[0], ROW_SUBTILE):
        rows = slice(r, r + ROW_SUBTILE)
        xn = _rms(x_ref[rows, :], g_ref[...], EPS).astype(BF16)
        proj = _dot(xn, win_ref[...])
        hy_ref[rows, :] = proj[:, :c0].astype(hy_ref.dtype)
        c = c_ref[rows, :]
        s_lo = slo_ref[rows, :]
        s_hi = shi_ref[rows, :]
        qn = _rms(proj[:, c0:c1], qg_ref[...], EPS).astype(BF16)
        q = _dot(qn, wuq_ref[...])
        kvn = _rms(proj[:, c1:c2], kvg_ref[...], EPS).astype(BF16)
        kn = _dot(kvn, wuk_ref[...])
        v_ref[rows, :] = (_dot(kvn, wuv_ref[...]) + ones_ref[...]).astype(BF16)
        k_pe = _rope(pltpu.roll(proj[:, c2:c2 + LANES], MLA_NOPE, 1), c, s_lo, s_hi, half)
        for h in range(MLA_HEADS):
            blk = slice(h * LANES, (h + 1) * LANES)
            q_ref[rows, blk] = (_rope(q[:, blk], c, s_lo, s_hi, half) * scale).astype(BF16)
            k_ref[rows, blk] = (kn[:, blk] + k_pe).astype(BF16)


def _mla_rope_tables(seq):
    inv = ROPE_THETA ** (-jnp.arange(0, MLA_ROPE, 2, dtype=F32) / MLA_ROPE)
    ang = jnp.arange(seq, dtype=F32)[:, None] * inv[None, :]
    cos, sin = jnp.cos(ang), jnp.sin(ang)
    half = MLA_ROPE // 2
    one = jnp.ones((seq, MLA_NOPE), F32)
    zero = jnp.zeros((seq, MLA_NOPE), F32)
    tail1 = jnp.ones((seq, LANES - MLA_QK), F32)
    tail0 = jnp.zeros((seq, LANES - MLA_QK), F32)
    zh = jnp.zeros((seq, half), F32)
    c = jnp.concatenate([one, cos, cos, tail1], axis=1)
    s_lo = jnp.concatenate([zero, -sin, zh, tail0], axis=1)
    s_hi = jnp.concatenate([zero, zh, sin, tail0], axis=1)
    return c, s_lo, s_hi


def _pad_heads(w, heads, width):
    rows = w.shape[0]
    w = w.reshape(rows, heads, width)
    return jnp.pad(w, ((0, 0), (0, 0), (0, LANES - width))).reshape(rows, heads * LANES)


def _even_in(x2d, g, w_in, q_norm, w_uq, kv_norm, w_ukv, seq):
    rows = x2d.shape[0]
    tm = ROW_TILE
    n_in = w_in.shape[1]
    n_pad = -(-(n_in + LANES - MLA_ROPE) // LANES) * LANES
    win = jnp.pad(w_in, ((0, 0), (0, n_pad - n_in))).astype(BF16)
    wuq = _pad_heads(w_uq, MLA_HEADS, MLA_QK).astype(BF16)
    wkv = w_ukv.reshape(MLA_KV_RANK, MLA_HEADS, MLA_NOPE + MLA_V)
    wuk = _pad_heads(wkv[:, :, :MLA_NOPE].reshape(MLA_KV_RANK, -1), MLA_HEADS, MLA_NOPE).astype(BF16)
    wuv = _pad_heads(wkv[:, :, MLA_NOPE:].reshape(MLA_KV_RANK, -1), MLA_HEADS, MLA_V).astype(BF16)
    c, s_lo, s_hi = _mla_rope_tables(seq)
    tiles_per_seq = seq // tm
    row_spec = lambda width: pl.BlockSpec((tm, width), lambda i: (i, 0))
    tab_spec = pl.BlockSpec((tm, LANES), lambda i: (i % tiles_per_seq, 0))
    hw = MLA_HEADS * LANES
    ones_col = jnp.tile((jnp.arange(LANES) == MLA_V).astype(F32), MLA_HEADS)[None, :]
    return pl.pallas_call(
        _even_in_kernel,
        grid=(rows // tm,),
        in_specs=[row_spec(D_MODEL), _const_spec((1, D_MODEL)), _const_spec(win.shape),
                  _const_spec((1, MLA_Q_RANK)), _const_spec(wuq.shape),
                  _const_spec((1, MLA_KV_RANK)), _const_spec(wuk.shape), _const_spec(wuv.shape),
                  _const_spec((1, hw)), tab_spec, tab_spec, tab_spec],
        out_specs=[row_spec(3 * HY_CH), row_spec(hw), row_spec(hw), row_spec(hw)],
        out_shape=[jax.ShapeDtypeStruct((rows, 3 * HY_CH), BF16)] + [jax.ShapeDtypeStruct((rows, hw), BF16)] * 3,
        compiler_params=_params("arbitrary"),
        name="even_in_proj",
    )(x2d, g[None, :], win, q_norm[None, :], wuq, kv_norm[None, :], wuk, wuv, ones_col, c, s_lo, s_hi)


def _attn_kernel(q_ref, k_ref, v_ref, o_ref, *, heads, sum_lane):
    first_half = lax.broadcasted_iota(jnp.int32, (q_ref.shape[0], LANES), 1) < LANES // 2
    outs = []
    for h in range(heads):
        blk = slice(h * LANES, (h + 1) * LANES)
        s = _dot_nt(q_ref[:, blk], k_ref[:, blk])
        e = jnp.exp2(s - jnp.max(s, axis=-1, keepdims=True)).astype(BF16)
        o = _dot(e, v_ref[:, blk])
        l = o[:, sum_lane:sum_lane + 1]
        outs.append(o / l)
        if h % 2 == 1:
            pair = jnp.where(first_half, outs[h - 1], pltpu.roll(outs[h], LANES // 2, 1))
            o_ref[:, (h // 2) * LANES:(h // 2 + 1) * LANES] = pair.astype(o_ref.dtype)


def _attention(q, k, v, batch, seq, heads, sum_lane):
    tq = ATTN_Q_TILE
    hw = heads * LANES
    ow = heads * LANES // 2
    q3, k3, v3 = (a.reshape(batch, seq, hw) for a in (q, k, v))
    q_spec = pl.BlockSpec((None, tq, hw), lambda b, i: (b, i, 0))
    kv_spec = pl.BlockSpec((None, seq, hw), lambda b, i: (b, 0, 0))
    out = pl.pallas_call(
        functools.partial(_attn_kernel, heads=heads, sum_lane=sum_lane),
        grid=(batch, seq // tq),
        in_specs=[q_spec, kv_spec, kv_spec],
        out_specs=pl.BlockSpec((None, tq, ow), lambda b, i: (b, i, 0)),
        out_shape=jax.ShapeDtypeStruct((batch, seq, ow), BF16),
        compiler_params=_params("arbitrary", "arbitrary"),
        name="mla_attention",
    )(q3, k3, v3)
    return out.reshape(batch * seq, ow)


def _odd_in_kernel(x_ref, g_ref, w_ref, c_ref, slo_ref, shi_ref, q_ref, k_ref, v_ref):
    half = DIFF_HD // 2
    scale = DIFF_HD ** -0.5 * LOG2E
    for r in range(0, x_ref.shape[0], ROW_SUBTILE):
        rows = slice(r, r + ROW_SUBTILE)
        xn = _rms(x_ref[rows, :], g_ref[...], EPS).astype(BF16)
        qkv = _dot(xn, w_ref[...])
        c = c_ref[rows, :]
        s_lo = slo_ref[rows, :]
        s_hi = shi_ref[rows, :]
        for h in range(DIFF_HEADS):
            blk = slice(h * LANES, (h + 1) * LANES)
            kblk = slice(D_MODEL + h * LANES, D_MODEL + (h + 1) * LANES)
            q_ref[rows, blk] = (_rope(qkv[:, blk], c, s_lo, s_hi, half) * scale).astype(BF16)
            k_ref[rows, blk] = _rope(qkv[:, kblk], c, s_lo, s_hi, half).astype(BF16)
        v_ref[rows, :] = qkv[:, 2 * D_MODEL:].astype(BF16)


def _diff_rope_tables(seq):
    inv = ROPE_THETA ** (-jnp.arange(0, DIFF_HD, 2, dtype=F32) / DIFF_HD)
    ang = jnp.arange(seq, dtype=F32)[:, None] * inv[None, :]
    cos, sin = jnp.cos(ang), jnp.sin(ang)
    zero = jnp.zeros_like(sin)
    reps = LANES // DIFF_HD
    c = jnp.tile(jnp.concatenate([cos, cos], axis=1), (1, reps))
    s_lo = jnp.tile(jnp.concatenate([-sin, zero], axis=1), (1, reps))
    s_hi = jnp.tile(jnp.concatenate([zero, sin], axis=1), (1, reps))
    return c, s_lo, s_hi


def _odd_in(x2d, g, w_qkv, seq):
    rows = x2d.shape[0]
    tm = ROW_TILE
    c, s_lo, s_hi = _diff_rope_tables(seq)
    tiles_per_seq = seq // tm
    row_spec = lambda width: pl.BlockSpec((tm, width), lambda i: (i, 0))
    tab_spec = pl.BlockSpec((tm, LANES), lambda i: (i % tiles_per_seq, 0))
    w = w_qkv.astype(BF16)
    return pl.pallas_call(
        _odd_in_kernel,
        grid=(rows // tm,),
        in_specs=[row_spec(D_MODEL), _const_spec((1, D_MODEL)), _const_spec(w.shape),
                  tab_spec, tab_spec, tab_spec],
        out_specs=[row_spec(D_MODEL)] * 3,
        out_shape=[jax.ShapeDtypeStruct((rows, D_MODEL), BF16)] * 3,
        compiler_params=_params("arbitrary"),
        name="odd_in_proj",
    )(x2d, g[None, :], w, c, s_lo, s_hi)


def _diff_attn_kernel(lam_ref, q_ref, k_ref, v_ref, g_ref, o_ref, s_ref, e_ref, *, lam_init):
    lp = lam_ref[...]
    lam = (jnp.exp(jnp.sum(lp[0:1] * lp[1:2], axis=-1, keepdims=True))
           - jnp.exp(jnp.sum(lp[2:3] * lp[3:4], axis=-1, keepdims=True)) + lam_init)
    g = g_ref[...] * (1.0 - lam_init)
    tq = q_ref.shape[0]
    first_head = lax.broadcasted_iota(jnp.int32, (tq, LANES), 1) < DIFF_HD
    blocks = [slice(r, r + SOFTMAX_ROWS) for r in range(0, tq, SOFTMAX_ROWS)]

    def scores(h):
        blk = slice(h * LANES, (h + 1) * LANES)
        q = q_ref[:, blk]
        k = k_ref[:, blk]
        zero = jnp.zeros_like(q)
        s_ref[h % 2, 0] = _dot_nt(jnp.where(first_head, q, zero), k)
        s_ref[h % 2, 1] = _dot_nt(jnp.where(first_head, zero, q), k)

    def numerator(slot, j):
        m = jnp.max(s_ref[slot, j], axis=-1, keepdims=True)
        sums = []
        for rows in blocks:
            m_rows = m[rows]
            acc = None
            for c0 in range(0, s_ref.shape[-1], SOFTMAX_KEYS):
                keys = slice(c0, c0 + SOFTMAX_KEYS)
                e = jnp.exp2(s_ref[slot, j, rows, keys] - m_rows)
                for l0 in range(0, SOFTMAX_KEYS, LANES):
                    part = e[:, l0:l0 + LANES]
                    acc = part if acc is None else acc + part
                e_ref[j, rows, keys] = e.astype(BF16)
            sums.append(jnp.sum(acc, axis=-1, keepdims=True))
        return jnp.concatenate(sums, axis=0)

    scores(0)
    for h in range(DIFF_HEADS):
        if h + 1 < DIFF_HEADS:
            scores(h + 1)
        blk = slice(h * LANES, (h + 1) * LANES)
        l1 = numerator(h % 2, 0)
        l2 = numerator(h % 2, 1)
        c = (lam * l1 / l2).astype(BF16)
        for rows in blocks:
            e_ref[0, rows, :] = e_ref[0, rows, :] - c[rows] * e_ref[1, rows, :]
        o = _dot(e_ref[0], v_ref[:, blk]) / l1
        o_ref[:, blk] = (o * lax.rsqrt(jnp.mean(o * o, axis=-1, keepdims=True) + SUBLN_EPS) * g).astype(o_ref.dtype)


def _diff_attention(q, k, v, lam_params, subln, lam_init, batch, seq):
    tq = ATTN_Q_TILE
    q3, k3, v3 = (a.reshape(batch, seq, D_MODEL) for a in (q, k, v))
    q_spec = pl.BlockSpec((None, tq, D_MODEL), lambda b, i: (b, i, 0))
    kv_spec = pl.BlockSpec((None, seq, D_MODEL), lambda b, i: (b, 0, 0))
    lam_tile = jnp.pad(jnp.stack(lam_params).astype(F32), ((0, 4), (0, LANES - DIFF_HD)))
    out = pl.pallas_call(
        functools.partial(_diff_attn_kernel, lam_init=lam_init),
        grid=(batch, seq // tq),
        in_specs=[pl.BlockSpec((8, LANES), lambda b, i: (0, 0)), q_spec, kv_spec, kv_spec,
                  pl.BlockSpec((1, LANES), lambda b, i: (0, 0))],
        out_specs=q_spec,
        out_shape=jax.ShapeDtypeStruct((batch, seq, D_MODEL), BF16),
        scratch_shapes=[pltpu.VMEM((2, 2, tq, seq), F32), pltpu.VMEM((2, tq, seq), BF16)],
        compiler_params=_params("arbitrary", "arbitrary"),
        name="diff_attention",
    )(lam_tile, q3, k3, v3, subln[None, :])
    return out.reshape(batch * seq, D_MODEL)


def _mem_kv_kernel(mem_ref, g_ref, wkv_ref, kt_ref, v_ref):
    mn = _rms(mem_ref[...], g_ref[...], EPS).astype(BF16)
    for layer in range(wkv_ref.shape[0]):
        kv = _dot(mn, wkv_ref[layer])
        kt_ref[layer] = kv[:, :D_MODEL].T.astype(BF16)
        v_ref[layer] = kv[:, D_MODEL:].astype(BF16)


def _mem_kv(mem, mem_norm, xa_wkv):
    batch, n_mem, _ = mem.shape
    depth = xa_wkv.shape[0]
    w = xa_wkv.astype(BF16)
    return pl.pallas_call(
        _mem_kv_kernel,
        grid=(batch,),
        in_specs=[pl.BlockSpec((None, n_mem, D_MODEL), lambda b: (b, 0, 0)), _const_spec((1, D_MODEL)),
                  _const_spec(w.shape)],
        out_specs=[pl.BlockSpec((depth, None, D_MODEL, n_mem), lambda b: (0, b, 0, 0)),
                   pl.BlockSpec((depth, None, n_mem, D_MODEL), lambda b: (0, b, 0, 0))],
        out_shape=[jax.ShapeDtypeStruct((depth, batch, D_MODEL, n_mem), BF16),
                   jax.ShapeDtypeStruct((depth, batch, n_mem, D_MODEL), BF16)],
        compiler_params=_params("arbitrary"),
        name="memory_kv",
    )(mem, mem_norm[None, :], w)


def _cross_mlp_kernel(*refs, n_mix, final_norm):
    x_ref, o_ref = refs[0], refs[-1]
    mix = refs[1:1 + 2 * n_mix]
    gc_ref, wq_ref, kt_ref, v_ref, wo_ref, gm_ref, wup_ref, wdn_ref, gf_ref = refs[1 + 2 * n_mix:-1]
    x = x_ref[...]
    for a_ref, w_ref in zip(mix[0::2], mix[1::2]):
        x = x + _dot(a_ref[...], w_ref[...])
    hn = _rms(x, gc_ref[...], EPS).astype(BF16)
    q = (_dot(hn, wq_ref[...]) * (X_HD ** -0.5 * LOG2E)).astype(BF16)
    heads = []
    for h in range(X_HEADS):
        blk = slice(h * X_HD, (h + 1) * X_HD)
        s = _dot(q[:, blk], kt_ref[blk, :])
        e = jnp.exp2(s - jnp.max(s, axis=-1, keepdims=True))
        l = jnp.sum(e, axis=-1, keepdims=True)
        heads.append((_dot(e.astype(BF16), v_ref[:, blk]) / l).astype(BF16))
    x = x + _dot(jnp.concatenate(heads, axis=1), wo_ref[...])
    hn = _rms(x, gm_ref[...], EPS).astype(BF16)
    u = jnp.maximum(_dot(hn, wup_ref[...]), 0.0)
    x = x + _dot((u * u).astype(BF16), wdn_ref[...])
    if final_norm:
        x = _rms(x, gf_ref[...], EPS)
    o_ref[...] = x


def _mix_cross_mlp(x2d, mix_pairs, g_cross, wq, kt, v, wo, g_mlp, w_up, w_down, g_final, seq, final_norm):
    rows = x2d.shape[0]
    tm = MLP_ROW_TILE
    tiles_per_seq = seq // tm
    n_mem = v.shape[1]
    row_spec = lambda width: pl.BlockSpec((tm, width), lambda i: (i, 0))
    vec_spec = _const_spec((1, D_MODEL))
    mix_specs, mix_args = [], []
    for a, w in mix_pairs:
        mix_specs += [row_spec(a.shape[1]), _const_spec(w.shape)]
        mix_args += [a, w]
    return pl.pallas_call(
        functools.partial(_cross_mlp_kernel, n_mix=len(mix_pairs), final_norm=final_norm),
        grid=(rows // tm,),
        in_specs=[row_spec(D_MODEL)] + mix_specs + [
            vec_spec, _const_spec((D_MODEL, D_MODEL)),
            pl.BlockSpec((None, D_MODEL, n_mem), lambda i: (i // tiles_per_seq, 0, 0)),
            pl.BlockSpec((None, n_mem, D_MODEL), lambda i: (i // tiles_per_seq, 0, 0)),
            _const_spec((D_MODEL, D_MODEL)), vec_spec, _const_spec((D_MODEL, D_FF)),
            _const_spec((D_FF, D_MODEL)), vec_spec],
        out_specs=row_spec(D_MODEL),
        out_shape=jax.ShapeDtypeStruct((rows, D_MODEL), F32),
        compiler_params=_params("arbitrary"),
        name="mix_cross_mlp",
    )(x2d, *mix_args, g_cross[None, :], wq.astype(BF16), kt, v, wo.astype(BF16), g_mlp[None, :],
      w_up.astype(BF16), w_down.astype(BF16), g_final[None, :])


def kernel(x, mem, ev_w_in, ev_conv_w, ev_conv_b, hy_w1, hy_b1, hy_w2, hy_b2, hy_w3, hy_freq, hy_skip,
           mla_q_norm, mla_w_uq, mla_kv_norm, mla_w_ukv, ev_w_out, od_w_qkv, dif_lq1, dif_lk1, dif_lq2,
           dif_lk2, dif_subln, od_w_out, norm_mix, norm_cross, norm_mlp, xa_wq, xa_wkv, xa_wo, mlp_up,
           mlp_down, mem_norm, final_norm):
    batch, seq, d = x.shape
    depth = norm_mix.shape[0]
    assert d == D_MODEL and seq % ROW_TILE == 0 and seq % MLP_ROW_TILE == 0 and seq % ATTN_Q_TILE == 0
    x2d = x.reshape(batch * seq, d)
    kt_all, v_all = _mem_kv(mem, mem_norm, xa_wkv)
    for i in range(depth):
        j = i // 2
        if i % 2 == 0:
            spectra = _hyena_spectra(seq, hy_w1[j], hy_b1[j], hy_w2[j], hy_b2[j], hy_w3[j], hy_freq[j])
            hy, q, k, v = _even_in(x2d, norm_mix[i], ev_w_in[j], mla_q_norm[j], mla_w_uq[j],
                                   mla_kv_norm[j], mla_w_ukv[j], seq)
            z = _hyena(hy, ev_conv_w[j], ev_conv_b[j], spectra, hy_skip[j], batch, seq)
            o = _attention(q, k, v, batch, seq, MLA_HEADS, sum_lane=MLA_V)
            w_out = ev_w_out[j].astype(BF16)
            mix = [(z.reshape(batch * seq, HY_CH), w_out[:HY_CH]), (o, w_out[HY_CH:])]
        else:
            lam_init = 0.8 - 0.6 * math.exp(-0.3 * i)
            q, k, v = _odd_in(x2d, norm_mix[i], od_w_qkv[j], seq)
            o = _diff_attention(q, k, v, (dif_lq1[j], dif_lk1[j], dif_lq2[j], dif_lk2[j]), dif_subln[j],
                                lam_init, batch, seq)
            mix = [(o, od_w_out[j].astype(BF16))]
        x2d = _mix_cross_mlp(x2d, mix, norm_cross[i], xa_wq[i], kt_all[i], v_all[i], xa_wo[i], norm_mlp[i],
                             mlp_up[i], mlp_down[i], final_norm, seq, final_norm=(i == depth - 1))
    return x2d.reshape(batch, seq, d)
```

```python
import functools
import math

import jax
import jax.numpy as jnp
import numpy as np
from jax import lax
from jax.experimental import pallas as pl
from jax.experimental.pallas import tpu as pltpu

F32 = jnp.float32
BF16 = jnp.bfloat16

D_MODEL = 1024
EPS = 1e-6
ROPE_THETA = 10000.0
HY_CH = 512
HY_ORDER = 2
HY_EMB = 33
HY_BANDS = (HY_EMB - 1) // 2
HY_FFN = 64
HY_FAST_PCT = 0.3
HY_SLOW_PCT = 1.5
HY_TARGET = 1e-2
MLA_HEADS = 8
MLA_NOPE = 64
MLA_ROPE = 32
MLA_V = 64
MLA_QK = MLA_NOPE + MLA_ROPE
MLA_Q_RANK = D_MODEL // 4
MLA_KV_RANK = D_MODEL // 8
DIFF_HEADS = 8
DIFF_HD = D_MODEL // DIFF_HEADS // 2
SUBLN_EPS = 1e-5
X_HEADS = 4
X_HD = D_MODEL // X_HEADS
D_FF = 4 * D_MODEL

LOG2E = math.log2(math.e)
LANES = 128
V7X_VMEM_BYTES = 64 * 1024 * 1024
VMEM_LIMIT = 60000 * 1024
assert VMEM_LIMIT < V7X_VMEM_BYTES

ROW_TILE = 1024
MLP_ROW_TILE = 1024
ROW_SUBTILE = 256
MLA_Q_TILE = 1024
DIFF_Q_TILE = 512
MEM_BATCH_TILE = 4
SOFTMAX_ROWS = 16
SOFTMAX_KEYS = 512
HY_CH_TILE = 256
HY_ROW_CHUNK = 512


def _const_spec(shape):
    nd = len(shape)
    return pl.BlockSpec(shape, lambda *_: (0,) * nd, pipeline_mode=pl.Buffered(1))


def _params(*sem):
    return pltpu.CompilerParams(dimension_semantics=sem, vmem_limit_bytes=VMEM_LIMIT)


def _rms(x, g, eps):
    return x * lax.rsqrt(jnp.mean(x * x, axis=-1, keepdims=True) + eps) * g


def _rope(x, c, s_lo, s_hi, half):
    return x * c + pltpu.roll(x, LANES - half, 1) * s_lo + pltpu.roll(x, half, 1) * s_hi


def _dot(a, b):
    return jnp.dot(a, b, preferred_element_type=F32)


def _dot_nt(a, b):
    return lax.dot_general(a, b, (((1,), (1,)), ((), ())), preferred_element_type=F32)


def _dot_f32(a, b):
    return jnp.dot(a, b, preferred_element_type=F32, precision=lax.Precision.HIGHEST)


def _filter_kernel(feats_ref, w1_ref, b1_ref, w2_ref, b2_ref, fr_ref, w3f_ref, w3b_ref, t_ref, dl_ref, sgn_ref,
                   cw_ref, sw_ref, cm_ref, sm_ref, k1r_ref, k1i_ref, k2r_ref, k2i_ref, act_ref, *, inv_scale):
    @pl.when((pl.program_id(0) == 0) & (pl.program_id(1) == 0))
    def _():
        fr = fr_ref[...]
        a = jnp.sin(fr * (_dot_f32(feats_ref[...], w1_ref[...]) + b1_ref[...]))
        act_ref[...] = jnp.sin(fr * (_dot_f32(a, w2_ref[...]) + b2_ref[...]))

    a = act_ref[...]
    window = jnp.exp(-t_ref[...] * jnp.abs(dl_ref[...]))
    h_f = _dot_f32(a, w3f_ref[...]) * window
    h_b = _dot_f32(a, w3b_ref[...]) * window
    row = lax.broadcasted_iota(jnp.int32, h_b.shape, 0)
    h_b = jnp.where(row == 0, 0.0, h_b)
    h = jnp.concatenate([h_f + h_b, h_f - h_b], axis=1)
    h = jnp.concatenate([h, h * sgn_ref[...]], axis=1).astype(BF16)
    tc = h_f.shape[1]
    col = lambda x, j: x[:, j * tc:(j + 1) * tc]
    for r in range(0, cm_ref.shape[0], HY_ROW_CHUNK):
        rows = slice(r, r + HY_ROW_CHUNK)
        c = _dot(cm_ref[rows, :], h)
        s = _dot(sm_ref[rows, :], h)
        cw, sw = cw_ref[0, rows, :], sw_ref[0, rows, :]
        k1r_ref[rows, :] = (cw * col(c, 0) + sw * col(s, 0)) * inv_scale
        k1i_ref[rows, :] = (sw * col(c, 1) - cw * col(s, 1)) * inv_scale
        cw, sw = cw_ref[1, rows, :], sw_ref[1, rows, :]
        k2r_ref[rows, :] = (cw * col(s, 2) + sw * col(c, 2)) * inv_scale
        k2i_ref[rows, :] = (sw * col(s, 3) - cw * col(c, 3)) * inv_scale


def _hyena_spectra(seq, hy_w1, hy_b1, hy_w2, hy_b2, hy_w3, hy_freq):
    f32 = F32
    n = 2 * seq
    half = seq // 2
    t = jnp.linspace(0.0, 1.0, seq, dtype=f32)[:, None]
    w = (2.0 * math.pi / seq) * jnp.arange(seq, dtype=f32)[:, None]
    bands = jnp.linspace(1e-4, HY_BANDS - 1, HY_BANDS, dtype=f32)[None, :]
    feats = jnp.concatenate([t, jnp.cos(bands * w), -jnp.sin(bands * w)], axis=-1)
    feats = jnp.pad(feats, ((0, 0), (0, LANES - HY_EMB)))
    pad_f = LANES - HY_FFN
    w1 = jnp.pad(hy_w1.astype(f32), ((0, LANES - HY_EMB), (0, pad_f)))
    b1 = jnp.pad(hy_b1.astype(f32), (0, pad_f))[None, :]
    w2 = jnp.pad(hy_w2.astype(f32), ((0, pad_f), (0, pad_f)))
    b2 = jnp.pad(hy_b2.astype(f32), (0, pad_f))[None, :]
    fr = jnp.pad(hy_freq.astype(f32), (0, pad_f))[None, :]
    w3 = jnp.pad(hy_w3.astype(f32), ((0, pad_f), (0, 0))).reshape(LANES, HY_ORDER, 2, HY_CH)
    w3f = w3[:, :, 0, :].reshape(LANES, HY_ORDER * HY_CH)
    w3b = w3[:, :, 1, :].reshape(LANES, HY_ORDER * HY_CH)
    max_decay = math.log(HY_TARGET) / HY_FAST_PCT
    min_decay = math.log(HY_TARGET) / HY_SLOW_PCT
    deltas = jnp.linspace(min_decay, max_decay, HY_CH, dtype=f32)[None, :]
    sgn = (1.0 - 2.0 * (jnp.arange(seq) % 2).astype(f32))[:, None]
    f_lo = jnp.arange(half, dtype=f32)
    half_w = (math.pi / n) * (jnp.stack([f_lo, seq - 1 - f_lo]) + 0.5)
    cw = jnp.cos(half_w)[:, :, None]
    sw = jnp.sin(half_w)[:, :, None]
    cmat, smat = _dft_matrices(seq)

    tc = HY_CH_TILE
    ncb = HY_CH // tc
    small = lambda shape: pl.BlockSpec(shape, lambda o, c: (0,) * len(shape))
    return pl.pallas_call(
        functools.partial(_filter_kernel, inv_scale=2.0 / n),
        grid=(HY_ORDER, ncb),
        in_specs=[
            small((seq, LANES)), small((LANES, LANES)), small((1, LANES)), small((LANES, LANES)),
            small((1, LANES)), small((1, LANES)),
            pl.BlockSpec((LANES, tc), lambda o, c: (0, o * ncb + c)),
            pl.BlockSpec((LANES, tc), lambda o, c: (0, o * ncb + c)),
            small((seq, 1)),
            pl.BlockSpec((1, tc), lambda o, c: (0, c)),
            small((seq, 1)), small((2, half, 1)), small((2, half, 1)),
            _const_spec((half, seq)), _const_spec((half, seq)),
        ],
        out_specs=[pl.BlockSpec((None, half, tc), lambda o, c: (o, 0, c))] * 4,
        out_shape=[jax.ShapeDtypeStruct((HY_ORDER, half, HY_CH), f32)] * 4,
        scratch_shapes=[pltpu.VMEM((seq, LANES), f32)],
        compiler_params=_params("arbitrary", "arbitrary"),
        name="hyena_filter_spectra",
    )(feats, w1, b1, w2, b2, fr, w3f, w3b, t, deltas, sgn, cw, sw, cmat, smat)


def _cos_sin_outer(row_mult, col_mult, modulus, blk=32):
    rows = row_mult.shape[0]
    step = row_mult[1] - row_mult[0]

    def table(mult):
        ang = ((mult[:, None] * col_mult[None, :]) % modulus).astype(F32) * (2.0 * math.pi / modulus)
        return jnp.cos(ang), jnp.sin(ang)

    ca, sa = (a[:, None, :] for a in table(blk * step * jnp.arange(rows // blk, dtype=jnp.int32)))
    cb, sb = (a[None, :, :] for a in table(row_mult[:blk]))
    shape = (rows, col_mult.shape[0])
    return (ca * cb - sa * sb).reshape(shape), (sa * cb + ca * sb).reshape(shape)


def _dft_matrices(seq):
    odd = 2 * jnp.arange(seq, dtype=jnp.int32) + 1
    cmat, smat = _cos_sin_outer(odd[:seq // 2], odd, 8 * seq)
    return cmat.astype(BF16), smat.astype(BF16)


def _half_dft_tables(seq, width):
    half = seq // 2
    odd_f = 2 * jnp.arange(half, dtype=jnp.int32) + 1
    ce, se = _cos_sin_outer(odd_f, jnp.arange(half, dtype=jnp.int32), 2 * seq)
    tw = []
    for p in range(2):
        ang_p = ((odd_f * (2 * p + 1)) % (8 * seq)).astype(F32) * (2.0 * math.pi / (8 * seq))
        tw += [jnp.cos(ang_p), jnp.sin(ang_p)]
    tw = jnp.broadcast_to(jnp.stack(tw)[:, :, None], (4, half, width))
    return ce.astype(BF16), se.astype(BF16), ce.T.astype(BF16), se.T.astype(BF16), tw


def _hyena_kernel(v_ref, x1_ref, x2_ref, wv_ref, wx1_ref, wx2_ref, bv_ref, bx1_ref, bx2_ref,
                  k1r_ref, k1i_ref, k2r_ref, k2i_ref, skip_ref, tw_ref, ce_ref, se_ref, cet_ref, set_ref,
                  o_ref, stage_ref, z_ref, pre_ref, pim_ref):
    seq = v_ref.shape[0]
    half = seq // 2
    row = lax.broadcasted_iota(jnp.int32, (half, v_ref.shape[1]), 0)
    first = row == 0
    last = row == half - 1
    parities = (pl.ds(0, half, stride=2), pl.ds(1, half, stride=2))
    lane_blocks = [slice(l, l + LANES) for l in range(0, v_ref.shape[1], LANES)]

    def deinterleave(u):
        for j, lanes in enumerate(lane_blocks):
            stage_ref[j] = u[:, lanes]
        return tuple(jnp.concatenate([stage_ref[j, par, :] for j in range(len(lane_blocks))], axis=1)
                     for par in parities)

    def short_conv(u_ref, w_ref, b_ref):
        ev, od = deinterleave(u_ref[...].astype(F32))
        w = w_ref[...]
        b = b_ref[...]
        od_prev = jnp.where(first, 0.0, pltpu.roll(od, 1, 0))
        ev_next = jnp.where(last, 0.0, pltpu.roll(ev, half - 1, 0))
        return (od_prev * w[0:1] + ev * w[1:2] + od * w[2:3] + b,
                ev * w[0:1] + od * w[1:2] + ev_next * w[2:3] + b)

    chunks = [slice(r, r + HY_ROW_CHUNK) for r in range(0, half, HY_ROW_CHUNK)]
    z0 = short_conv(v_ref, wv_ref, bv_ref)
    z_ref[0] = z0[0]
    z_ref[1] = z0[1]
    gate_refs = ((x1_ref, wx1_ref, bx1_ref), (x2_ref, wx2_ref, bx2_ref))
    skip = skip_ref[...]
    for o in range(HY_ORDER):
        zb = (z_ref[0].astype(BF16), z_ref[1].astype(BF16))
        for rows in chunks:
            ce = ce_ref[rows, :]
            se = se_ref[rows, :]
            r_, i_ = [], []
            for p in range(2):
                a = _dot(ce, zb[p])
                b = _dot(se, zb[p])
                c, s = tw_ref[2 * p, rows, :], tw_ref[2 * p + 1, rows, :]
                r_.append(c * a - s * b)
                i_.append(c * b + s * a)
            z1r, z1i = r_[0] + r_[1], -(i_[0] + i_[1])
            z2r, z2i = i_[0] - i_[1], r_[1] - r_[0]
            k1r, k1i = k1r_ref[o, rows, :], k1i_ref[o, rows, :]
            k2r, k2i = k2r_ref[o, rows, :], k2i_ref[o, rows, :]
            y1r, y1i = z1r * k1r - z1i * k1i, z1r * k1i + z1i * k1r
            y2r, y2i = z2r * k2r - z2i * k2i, z2r * k2i + z2i * k2r
            q = ((y1r - y2i, y1i - y2r), (y1r + y2i, y1i + y2r))
            for p in range(2):
                c, s = tw_ref[2 * p, rows, :], tw_ref[2 * p + 1, rows, :]
                pre_ref[p, rows, :] = (c * q[p][0] - s * q[p][1]).astype(BF16)
                pim_ref[p, rows, :] = (-(c * q[p][1] + s * q[p][0])).astype(BF16)
        gate = short_conv(*gate_refs[o])
        for p in range(2):
            pre = pre_ref[p]
            pim = pim_ref[p]
            for rows in chunks:
                y = _dot(cet_ref[rows, :], pre) + _dot(set_ref[rows, :], pim)
                z_ref[p, rows, :] = gate[p][rows] * (y + skip[o:o + 1] * z_ref[p, rows, :])
    for j, lanes in enumerate(lane_blocks):
        stage_ref[j, parities[0], :] = z_ref[0, :, lanes]
        stage_ref[j, parities[1], :] = z_ref[1, :, lanes]
        o_ref[:, lanes] = stage_ref[j].astype(o_ref.dtype)


def _hyena(hy, conv_w, conv_b, spectra, skip, batch, seq):
    tc = HY_CH_TILE
    ncb = HY_CH // tc
    half = seq // 2
    hy3 = hy.reshape(batch, seq, 3 * HY_CH)
    cb = conv_b[None, :]
    ce, se, cet, set_, tw = _half_dft_tables(seq, tc)
    k1r, k1i, k2r, k2i = spectra
    u_spec = lambda g: pl.BlockSpec((None, seq, tc), lambda c, b: (b, 0, g * ncb + c))
    w_spec = lambda g: pl.BlockSpec((3, tc), lambda c, b: (0, g * ncb + c))
    b_spec = lambda g: pl.BlockSpec((1, tc), lambda c, b: (0, g * ncb + c))
    k_spec = pl.BlockSpec((HY_ORDER, half, tc), lambda c, b: (0, 0, c), pipeline_mode=pl.Buffered(1))
    return pl.pallas_call(
        _hyena_kernel,
        grid=(ncb, batch),
        in_specs=[u_spec(0), u_spec(1), u_spec(2), w_spec(0), w_spec(1), w_spec(2),
                  b_spec(0), b_spec(1), b_spec(2), k_spec, k_spec, k_spec, k_spec,
                  pl.BlockSpec((HY_ORDER, tc), lambda c, b: (0, c)),
                  _const_spec((4, half, tc)),
                  _const_spec((half, half)), _const_spec((half, half)),
                  _const_spec((half, half)), _const_spec((half, half))],
        out_specs=pl.BlockSpec((None, seq, tc), lambda c, b: (b, 0, c)),
        out_shape=jax.ShapeDtypeStruct((batch, seq, HY_CH), BF16),
        scratch_shapes=[pltpu.VMEM((tc // LANES, seq, LANES), F32), pltpu.VMEM((2, half, tc), F32),
                        pltpu.VMEM((2, half, tc), BF16), pltpu.VMEM((2, half, tc), BF16)],
        compiler_params=_params("arbitrary", "arbitrary"),
        name="hyena_long_conv",
    )(hy3, hy3, hy3, conv_w, conv_w, conv_w, cb, cb, cb, k1r, k1i, k2r, k2i, skip, tw, ce, se, cet, set_)


def _even_in_kernel(x_ref, g_ref, win_ref, qg_ref, wuq_ref, kvg_ref, wuk_ref, wuv_ref, ones_ref,
                    c_ref, slo_ref, shi_ref, hy_ref, q_ref, k_ref, v_ref):
    c0 = 3 * HY_CH
    c1 = c0 + MLA_Q_RANK
    c2 = c1 + MLA_KV_RANK
    half = MLA_ROPE // 2
    scale = MLA_QK ** -0.5 * LOG2E
    for r in range(0, x_ref.shape[0], ROW_SUBTILE):
        rows = slice(r, r + ROW_SUBTILE)
        xn = _rms(x_ref[rows, :], g_ref[...], EPS).astype(BF16)
        proj = _dot(xn, win_ref[...])
        hy_ref[rows, :] = proj[:, :c0].astype(hy_ref.dtype)
        c = c_ref[rows, :]
        s_lo = slo_ref[rows, :]
        s_hi = shi_ref[rows, :]
        qn = _rms(proj[:, c0:c1], qg_ref[...], EPS).astype(BF16)
        q = _dot(qn, wuq_ref[...])
        kvn = _rms(proj[:, c1:c2], kvg_ref[...], EPS).astype(BF16)
        kn = _dot(kvn, wuk_ref[...])
        v_ref[rows, :] = (_dot(kvn, wuv_ref[...]) + ones_ref[...]).astype(BF16)
        k_pe = _rope(pltpu.roll(proj[:, c2:c2 + LANES], MLA_NOPE, 1), c, s_lo, s_hi, half)
        for h in range(MLA_HEADS):
            blk = slice(h * LANES, (h + 1) * LANES)
            q_ref[rows, blk] = (_rope(q[:, blk], c, s_lo, s_hi, half) * scale).astype(BF16)
            k_ref[rows, blk] = (kn[:, blk] + k_pe).astype(BF16)


def _mla_rope_tables(seq):
    inv = ROPE_THETA ** (-jnp.arange(0, MLA_ROPE, 2, dtype=F32) / MLA_ROPE)
    ang = jnp.arange(seq, dtype=F32)[:, None] * inv[None, :]
    cos, sin = jnp.cos(ang), jnp.sin(ang)
    half = MLA_ROPE // 2
    one = jnp.ones((seq, MLA_NOPE), F32)
    zero = jnp.zeros((seq, MLA_NOPE), F32)
    tail1 = jnp.ones((seq, LANES - MLA_QK), F32)
    tail0 = jnp.zeros((seq, LANES - MLA_QK), F32)
    zh = jnp.zeros((seq, half), F32)
    c = jnp.concatenate([one, cos, cos, tail1], axis=1)
    s_lo = jnp.concatenate([zero, -sin, zh, tail0], axis=1)
    s_hi = jnp.concatenate([zero, zh, sin, tail0], axis=1)
    return c, s_lo, s_hi


def _pad_heads(w, heads, width):
    rows = w.shape[0]
    w = w.reshape(rows, heads, width)
    return jnp.pad(w, ((0, 0), (0, 0), (0, LANES - width))).reshape(rows, heads * LANES)


def _even_in(x2d, g, w_in, q_norm, w_uq, kv_norm, w_ukv, seq):
    rows = x2d.shape[0]
    tm = ROW_TILE
    n_in = w_in.shape[1]
    n_pad = -(-(n_in + LANES - MLA_ROPE) // LANES) * LANES
    win = jnp.pad(w_in, ((0, 0), (0, n_pad - n_in))).astype(BF16)
    wuq = _pad_heads(w_uq, MLA_HEADS, MLA_QK).astype(BF16)
    wkv = w_ukv.reshape(MLA_KV_RANK, MLA_HEADS, MLA_NOPE + MLA_V)
    wuk = _pad_heads(wkv[:, :, :MLA_NOPE].reshape(MLA_KV_RANK, -1), MLA_HEADS, MLA_NOPE).astype(BF16)
    wuv = _pad_heads(wkv[:, :, MLA_NOPE:].reshape(MLA_KV_RANK, -1), MLA_HEADS, MLA_V).astype(BF16)
    c, s_lo, s_hi = _mla_rope_tables(seq)
    tiles_per_seq = seq // tm
    row_spec = lambda width: pl.BlockSpec((tm, width), lambda i: (i, 0))
    tab_spec = pl.BlockSpec((tm, LANES), lambda i: (i % tiles_per_seq, 0))
    hw = MLA_HEADS * LANES
    ones_col = jnp.tile((jnp.arange(LANES) == MLA_V).astype(F32), MLA_HEADS)[None, :]
    return pl.pallas_call(
        _even_in_kernel,
        grid=(rows // tm,),
        in_specs=[row_spec(D_MODEL), _const_spec((1, D_MODEL)), _const_spec(win.shape),
                  _const_spec((1, MLA_Q_RANK)), _const_spec(wuq.shape),
                  _const_spec((1, MLA_KV_RANK)), _const_spec(wuk.shape), _const_spec(wuv.shape),
                  _const_spec((1, hw)), tab_spec, tab_spec, tab_spec],
        out_specs=[row_spec(3 * HY_CH), row_spec(hw), row_spec(hw), row_spec(hw)],
        out_shape=[jax.ShapeDtypeStruct((rows, 3 * HY_CH), BF16)] + [jax.ShapeDtypeStruct((rows, hw), BF16)] * 3,
        compiler_params=_params("arbitrary"),
        name="even_in_proj",
    )(x2d, g[None, :], win, q_norm[None, :], wuq, kv_norm[None, :], wuk, wuv, ones_col, c, s_lo, s_hi)


def _attn_kernel(q_ref, k_ref, v_ref, o_ref, *, heads, sum_lane):
    first_half = lax.broadcasted_iota(jnp.int32, (q_ref.shape[0], LANES), 1) < LANES // 2
    outs = []
    for h in range(heads):
        blk = slice(h * LANES, (h + 1) * LANES)
        s = _dot_nt(q_ref[:, blk], k_ref[:, blk])
        e = jnp.exp2(s - jnp.max(s, axis=-1, keepdims=True)).astype(BF16)
        o = _dot(e, v_ref[:, blk])
        l = o[:, sum_lane:sum_lane + 1]
        outs.append(o / l)
        if h % 2 == 1:
            pair = jnp.where(first_half, outs[h - 1], pltpu.roll(outs[h], LANES // 2, 1))
            o_ref[:, (h // 2) * LANES:(h // 2 + 1) * LANES] = pair.astype(o_ref.dtype)


def _attention(q, k, v, batch, seq, heads, sum_lane):
    tq = MLA_Q_TILE
    hw = heads * LANES
    ow = heads * LANES // 2
    q3, k3, v3 = (a.reshape(batch, seq, hw) for a in (q, k, v))
    q_spec = pl.BlockSpec((None, tq, hw), lambda b, i: (b, i, 0))
    kv_spec = pl.BlockSpec((None, seq, hw), lambda b, i: (b, 0, 0))
    out = pl.pallas_call(
        functools.partial(_attn_kernel, heads=heads, sum_lane=sum_lane),
        grid=(batch, seq // tq),
        in_specs=[q_spec, kv_spec, kv_spec],
        out_specs=pl.BlockSpec((None, tq, ow), lambda b, i: (b, i, 0)),
        out_shape=jax.ShapeDtypeStruct((batch, seq, ow), BF16),
        compiler_params=_params("arbitrary", "arbitrary"),
        name="mla_attention",
    )(q3, k3, v3)
    return out.reshape(batch * seq, ow)


def _odd_in_kernel(x_ref, g_ref, w_ref, c_ref, slo_ref, shi_ref, q_ref, k_ref, v_ref):
    half = DIFF_HD // 2
    scale = DIFF_HD ** -0.5 * LOG2E
    for r in range(0, x_ref.shape[0], ROW_SUBTILE):
        rows = slice(r, r + ROW_SUBTILE)
        xn = _rms(x_ref[rows, :], g_ref[...], EPS).astype(BF16)
        qkv = _dot(xn, w_ref[...])
        c = c_ref[rows, :]
        s_lo = slo_ref[rows, :]
        s_hi = shi_ref[rows, :]
        for h in range(DIFF_HEADS):
            blk = slice(h * LANES, (h + 1) * LANES)
            kblk = slice(D_MODEL + h * LANES, D_MODEL + (h + 1) * LANES)
            q_ref[rows, blk] = (_rope(qkv[:, blk], c, s_lo, s_hi, half) * scale).astype(BF16)
            k_ref[rows, blk] = _rope(qkv[:, kblk], c, s_lo, s_hi, half).astype(BF16)
        v_ref[rows, :] = qkv[:, 2 * D_MODEL:].astype(BF16)


def _diff_rope_tables(seq):
    inv = ROPE_THETA ** (-jnp.arange(0, DIFF_HD, 2, dtype=F32) / DIFF_HD)
    ang = jnp.arange(seq, dtype=F32)[:, None] * inv[None, :]
    cos, sin = jnp.cos(ang), jnp.sin(ang)
    zero = jnp.zeros_like(sin)
    reps = LANES // DIFF_HD
    c = jnp.tile(jnp.concatenate([cos, cos], axis=1), (1, reps))
    s_lo = jnp.tile(jnp.concatenate([-sin, zero], axis=1), (1, reps))
    s_hi = jnp.tile(jnp.concatenate([zero, sin], axis=1), (1, reps))
    return c, s_lo, s_hi


def _odd_in(x2d, g, w_qkv, seq):
    rows = x2d.shape[0]
    tm = ROW_TILE
    c, s_lo, s_hi = _diff_rope_tables(seq)
    tiles_per_seq = seq // tm
    row_spec = lambda width: pl.BlockSpec((tm, width), lambda i: (i, 0))
    tab_spec = pl.BlockSpec((tm, LANES), lambda i: (i % tiles_per_seq, 0))
    w = w_qkv.astype(BF16)
    return pl.pallas_call(
        _odd_in_kernel,
        grid=(rows // tm,),
        in_specs=[row_spec(D_MODEL), _const_spec((1, D_MODEL)), _const_spec(w.shape),
                  tab_spec, tab_spec, tab_spec],
        out_specs=[row_spec(D_MODEL)] * 3,
        out_shape=[jax.ShapeDtypeStruct((rows, D_MODEL), BF16)] * 3,
        compiler_params=_params("arbitrary"),
        name="odd_in_proj",
    )(x2d, g[None, :], w, c, s_lo, s_hi)


def _diff_attn_kernel(lam_ref, q_ref, k_ref, v_ref, g_ref, o_ref, s_ref, e_ref, *, lam_init):
    lp = lam_ref[...]
    lam = (jnp.exp(jnp.sum(lp[0:1] * lp[1:2], axis=-1, keepdims=True))
           - jnp.exp(jnp.sum(lp[2:3] * lp[3:4], axis=-1, keepdims=True)) + lam_init)
    g = g_ref[...] * (1.0 - lam_init)
    tq = q_ref.shape[0]
    first_head = lax.broadcasted_iota(jnp.int32, (tq, LANES), 1) < DIFF_HD
    blocks = [slice(r, r + SOFTMAX_ROWS) for r in range(0, tq, SOFTMAX_ROWS)]

    def scores(h):
        blk = slice(h * LANES, (h + 1) * LANES)
        q = q_ref[:, blk]
        k = k_ref[:, blk]
        zero = jnp.zeros_like(q)
        s_ref[h % 2, 0] = _dot_nt(jnp.where(first_head, q, zero), k)
        s_ref[h % 2, 1] = _dot_nt(jnp.where(first_head, zero, q), k)

    def numerator(slot, j):
        m = jnp.max(s_ref[slot, j], axis=-1, keepdims=True)
        sums = []
        for rows in blocks:
            m_rows = m[rows]
            acc = None
            for c0 in range(0, s_ref.shape[-1], SOFTMAX_KEYS):
                keys = slice(c0, c0 + SOFTMAX_KEYS)
                e = jnp.exp2(s_ref[slot, j, rows, keys] - m_rows)
                for l0 in range(0, SOFTMAX_KEYS, LANES):
                    part = e[:, l0:l0 + LANES]
                    acc = part if acc is None else acc + part
                e_ref[j, rows, keys] = e.astype(BF16)
            sums.append(jnp.sum(acc, axis=-1, keepdims=True))
        return jnp.concatenate(sums, axis=0)

    scores(0)
    for h in range(DIFF_HEADS):
        if h + 1 < DIFF_HEADS:
            scores(h + 1)
        blk = slice(h * LANES, (h + 1) * LANES)
        l1 = numerator(h % 2, 0)
        l2 = numerator(h % 2, 1)
        c = (lam * l1 / l2).astype(BF16)
        for rows in blocks:
            e_ref[0, rows, :] = e_ref[0, rows, :] - c[rows] * e_ref[1, rows, :]
        o = _dot(e_ref[0], v_ref[:, blk]) / l1
        o_ref[:, blk] = (o * lax.rsqrt(jnp.mean(o * o, axis=-1, keepdims=True) + SUBLN_EPS) * g).astype(o_ref.dtype)


def _diff_attention(q, k, v, lam_params, subln, lam_init, batch, seq):
    tq = DIFF_Q_TILE
    q3, k3, v3 = (a.reshape(batch, seq, D_MODEL) for a in (q, k, v))
    q_spec = pl.BlockSpec((None, tq, D_MODEL), lambda b, i: (b, i, 0))
    kv_spec = pl.BlockSpec((None, seq, D_MODEL), lambda b, i: (b, 0, 0))
    lam_tile = jnp.pad(jnp.stack(lam_params).astype(F32), ((0, 4), (0, LANES - DIFF_HD)))
    out = pl.pallas_call(
        functools.partial(_diff_attn_kernel, lam_init=lam_init),
        grid=(batch, seq // tq),
        in_specs=[pl.BlockSpec((8, LANES), lambda b, i: (0, 0)), q_spec, kv_spec, kv_spec,
                  pl.BlockSpec((1, LANES), lambda b, i: (0, 0))],
        out_specs=q_spec,
        out_shape=jax.ShapeDtypeStruct((batch, seq, D_MODEL), BF16),
        scratch_shapes=[pltpu.VMEM((2, 2, tq, seq), F32), pltpu.VMEM((2, tq, seq), BF16)],
        compiler_params=_params("arbitrary", "arbitrary"),
        name="diff_attention",
    )(lam_tile, q3, k3, v3, subln[None, :])
    return out.reshape(batch * seq, D_MODEL)


def _mem_kv_kernel(mem_ref, g_ref, wkv_ref, kt_ref, v_ref):
    nb, n_mem, _ = mem_ref.shape
    mn = _rms(mem_ref[...].reshape(nb * n_mem, D_MODEL), g_ref[...], EPS).astype(BF16)
    for layer in range(wkv_ref.shape[0]):
        kv = _dot(mn, wkv_ref[layer])
        for b in range(nb):
            rows = slice(b * n_mem, (b + 1) * n_mem)
            kt_ref[layer, b] = kv[rows, :D_MODEL].T.astype(BF16)
            v_ref[layer, b] = kv[rows, D_MODEL:].astype(BF16)


def _mem_kv(mem, mem_norm, xa_wkv):
    batch, n_mem, _ = mem.shape
    depth = xa_wkv.shape[0]
    w = xa_wkv.astype(BF16)
    nb = MEM_BATCH_TILE
    return pl.pallas_call(
        _mem_kv_kernel,
        grid=(batch // nb,),
        in_specs=[pl.BlockSpec((nb, n_mem, D_MODEL), lambda b: (b, 0, 0)), _const_spec((1, D_MODEL)),
                  _const_spec(w.shape)],
        out_specs=[pl.BlockSpec((depth, nb, D_MODEL, n_mem), lambda b: (0, b, 0, 0)),
                   pl.BlockSpec((depth, nb, n_mem, D_MODEL), lambda b: (0, b, 0, 0))],
        out_shape=[jax.ShapeDtypeStruct((depth, batch, D_MODEL, n_mem), BF16),
                   jax.ShapeDtypeStruct((depth, batch, n_mem, D_MODEL), BF16)],
        compiler_params=_params("arbitrary"),
        name="memory_kv",
    )(mem, mem_norm[None, :], w)


def _cross_mlp_kernel(*refs, n_mix, final_norm):
    x_ref, o_ref = refs[0], refs[-1]
    mix = refs[1:1 + 2 * n_mix]
    gc_ref, wq_ref, kt_ref, v_ref, wo_ref, gm_ref, wup_ref, wdn_ref, gf_ref = refs[1 + 2 * n_mix:-1]
    x = x_ref[...]
    for a_ref, w_ref in zip(mix[0::2], mix[1::2]):
        x = x + _dot(a_ref[...], w_ref[...])
    hn = _rms(x, gc_ref[...], EPS).astype(BF16)
    q = (_dot(hn, wq_ref[...]) * (X_HD ** -0.5 * LOG2E)).astype(BF16)
    heads = []
    for h in range(X_HEADS):
        blk = slice(h * X_HD, (h + 1) * X_HD)
        s = _dot(q[:, blk], kt_ref[blk, :])
        e = jnp.exp2(s - jnp.max(s, axis=-1, keepdims=True))
        l = jnp.sum(e, axis=-1, keepdims=True)
        heads.append((_dot(e.astype(BF16), v_ref[:, blk]) / l).astype(BF16))
    x = x + _dot(jnp.concatenate(heads, axis=1), wo_ref[...])
    hn = _rms(x, gm_ref[...], EPS).astype(BF16)
    u = jnp.maximum(_dot(hn, wup_ref[...]), 0.0)
    x = x + _dot((u * u).astype(BF16), wdn_ref[...])
    if final_norm:
        x = _rms(x, gf_ref[...], EPS)
    o_ref[...] = x


def _mix_cross_mlp(x2d, mix_pairs, g_cross, wq, kt, v, wo, g_mlp, w_up, w_down, g_final, seq, final_norm):
    rows = x2d.shape[0]
    tm = MLP_ROW_TILE
    tiles_per_seq = seq // tm
    n_mem = v.shape[1]
    row_spec = lambda width: pl.BlockSpec((tm, width), lambda i: (i, 0))
    vec_spec = _const_spec((1, D_MODEL))
    mix_specs, mix_args = [], []
    for a, w in mix_pairs:
        mix_specs += [row_spec(a.shape[1]), _const_spec(w.shape)]
        mix_args += [a, w]
    return pl.pallas_call(
        functools.partial(_cross_mlp_kernel, n_mix=len(mix_pairs), final_norm=final_norm),
        grid=(rows // tm,),
        in_specs=[row_spec(D_MODEL)] + mix_specs + [
            vec_spec, _const_spec((D_MODEL, D_MODEL)),
            pl.BlockSpec((None, D_MODEL, n_mem), lambda i: (i // tiles_per_seq, 0, 0)),
            pl.BlockSpec((None, n_mem, D_MODEL), lambda i: (i // tiles_per_seq, 0, 0)),
            _const_spec((D_MODEL, D_MODEL)), vec_spec, _const_spec((D_MODEL, D_FF)),
            _const_spec((D_FF, D_MODEL)), vec_spec],
        out_specs=row_spec(D_MODEL),
        out_shape=jax.ShapeDtypeStruct((rows, D_MODEL), F32),
        compiler_params=_params("arbitrary"),
        name="mix_cross_mlp",
    )(x2d, *mix_args, g_cross[None, :], wq.astype(BF16), kt, v, wo.astype(BF16), g_mlp[None, :],
      w_up.astype(BF16), w_down.astype(BF16), g_final[None, :])


def kernel(x, mem, ev_w_in, ev_conv_w, ev_conv_b, hy_w1, hy_b1, hy_w2, hy_b2, hy_w3, hy_freq, hy_skip,
           mla_q_norm, mla_w_uq, mla_kv_norm, mla_w_ukv, ev_w_out, od_w_qkv, dif_lq1, dif_lk1, dif_lq2,
           dif_lk2, dif_subln, od_w_out, norm_mix, norm_cross, norm_mlp, xa_wq, xa_wkv, xa_wo, mlp_up,
           mlp_down, mem_norm, final_norm):
    batch, seq, d = x.shape
    depth = norm_mix.shape[0]
    assert d == D_MODEL and all(seq % t == 0 for t in (ROW_TILE, MLP_ROW_TILE, MLA_Q_TILE, DIFF_Q_TILE))
    assert batch % MEM_BATCH_TILE == 0
    x2d = x.reshape(batch * seq, d)
    kt_all, v_all = _mem_kv(mem, mem_norm, xa_wkv)
    for i in range(depth):
        j = i // 2
        if i % 2 == 0:
            spectra = _hyena_spectra(seq, hy_w1[j], hy_b1[j], hy_w2[j], hy_b2[j], hy_w3[j], hy_freq[j])
            hy, q, k, v = _even_in(x2d, norm_mix[i], ev_w_in[j], mla_q_norm[j], mla_w_uq[j],
                                   mla_kv_norm[j], mla_w_ukv[j], seq)
            z = _hyena(hy, ev_conv_w[j], ev_conv_b[j], spectra, hy_skip[j], batch, seq)
            o = _attention(q, k, v, batch, seq, MLA_HEADS, sum_lane=MLA_V)
            w_out = ev_w_out[j].astype(BF16)
            mix = [(z.reshape(batch * seq, HY_CH), w_out[:HY_CH]), (o, w_out[HY_CH:])]
        else:
            lam_init = 0.8 - 0.6 * math.exp(-0.3 * i)
            q, k, v = _odd_in(x2d, norm_mix[i], od_w_qkv[j], seq)
            o = _diff_attention(q, k, v, (dif_lq1[j], dif_lk1[j], dif_lq2[j], dif_lk2[j]), dif_subln[j],
                                lam_init, batch, seq)
            mix = [(o, od_w_out[j].astype(BF16))]
        x2d = _mix_cross_mlp(x2d, mix, norm_cross[i], xa_wq[i], kt_all[i], v_all[i], xa_wo[i], norm_mlp[i],
                             mlp_up[i], mlp_down[i], final_norm, seq, final_norm=(i == depth - 1))
    return x2d.reshape(batch, seq, d)
```

```python
import functools
import math

import jax
import jax.numpy as jnp
from jax import lax
from jax.experimental import pallas as pl
from jax.experimental.pallas import tpu as pltpu

F32 = jnp.float32
BF16 = jnp.bfloat16

D_MODEL = 1024
EPS = 1e-6
ROPE_THETA = 10000.0
HY_CH = 512
HY_ORDER = 2
HY_EMB = 33
HY_BANDS = (HY_EMB - 1) // 2
HY_FFN = 64
HY_FAST_PCT = 0.3
HY_SLOW_PCT = 1.5
HY_TARGET = 1e-2
MLA_HEADS = 8
MLA_NOPE = 64
MLA_ROPE = 32
MLA_V = 64
MLA_QK = MLA_NOPE + MLA_ROPE
MLA_Q_RANK = D_MODEL // 4
MLA_KV_RANK = D_MODEL // 8
DIFF_HEADS = 8
DIFF_HD = D_MODEL // DIFF_HEADS // 2
SUBLN_EPS = 1e-5
X_HEADS = 4
X_HD = D_MODEL // X_HEADS
D_FF = 4 * D_MODEL

LOG2E = math.log2(math.e)
LANES = 128
V7X_VMEM_BYTES = 64 * 1024 * 1024
VMEM_LIMIT = 60000 * 1024
assert VMEM_LIMIT < V7X_VMEM_BYTES

ROW_TILE = 1024
MLP_ROW_TILE = 1024
ROW_SUBTILE = 256
MLA_Q_TILE = 1024
DIFF_Q_TILE = 512
MEM_BATCH_TILE = 4
SOFTMAX_ROWS = 16
SOFTMAX_KEYS = 512
HY_CH_TILE = 256
HY_ROW_CHUNK = 512


def _const_spec(shape):
    nd = len(shape)
    return pl.BlockSpec(shape, lambda *_: (0,) * nd, pipeline_mode=pl.Buffered(1))


def _params(*sem):
    return pltpu.CompilerParams(dimension_semantics=sem, vmem_limit_bytes=VMEM_LIMIT)


def _rms(x, g, eps):
    return x * lax.rsqrt(jnp.mean(x * x, axis=-1, keepdims=True) + eps) * g


def _rope(x, c, s_lo, s_hi, half):
    return x * c + pltpu.roll(x, LANES - half, 1) * s_lo + pltpu.roll(x, half, 1) * s_hi


def _dot(a, b):
    return jnp.dot(a, b, preferred_element_type=F32)


def _dot_nt(a, b):
    return lax.dot_general(a, b, (((1,), (1,)), ((), ())), preferred_element_type=F32)


def _dot_f32(a, b):
    return jnp.dot(a, b, preferred_element_type=F32, precision=lax.Precision.HIGHEST)


def _filter_kernel(feats_ref, w1_ref, b1_ref, w2_ref, b2_ref, fr_ref, w3f_ref, w3b_ref, t_ref, dl_ref, sgn_ref,
                   cw_ref, sw_ref, cm_ref, sm_ref, k1r_ref, k1i_ref, k2r_ref, k2i_ref, act_ref, *, inv_scale):
    @pl.when((pl.program_id(0) == 0) & (pl.program_id(1) == 0))
    def _():
        fr = fr_ref[...]
        a = jnp.sin(fr * (_dot_f32(feats_ref[...], w1_ref[...]) + b1_ref[...]))
        act_ref[...] = jnp.sin(fr * (_dot_f32(a, w2_ref[...]) + b2_ref[...]))

    a = act_ref[...]
    window = jnp.exp(-t_ref[...] * jnp.abs(dl_ref[...]))
    h_f = _dot_f32(a, w3f_ref[...]) * window
    h_b = _dot_f32(a, w3b_ref[...]) * window
    row = lax.broadcasted_iota(jnp.int32, h_b.shape, 0)
    h_b = jnp.where(row == 0, 0.0, h_b)
    h = jnp.concatenate([h_f + h_b, h_f - h_b], axis=1)
    h = jnp.concatenate([h, h * sgn_ref[...]], axis=1).astype(BF16)
    tc = h_f.shape[1]
    col = lambda x, j: x[:, j * tc:(j + 1) * tc]
    for r in range(0, cm_ref.shape[0], HY_ROW_CHUNK):
        rows = slice(r, r + HY_ROW_CHUNK)
        c = _dot(cm_ref[rows, :], h)
        s = _dot(sm_ref[rows, :], h)
        cw, sw = cw_ref[0, rows, :], sw_ref[0, rows, :]
        k1r_ref[rows, :] = (cw * col(c, 0) + sw * col(s, 0)) * inv_scale
        k1i_ref[rows, :] = (sw * col(c, 1) - cw * col(s, 1)) * inv_scale
        cw, sw = cw_ref[1, rows, :], sw_ref[1, rows, :]
        k2r_ref[rows, :] = (cw * col(s, 2) + sw * col(c, 2)) * inv_scale
        k2i_ref[rows, :] = (sw * col(s, 3) - cw * col(c, 3)) * inv_scale


def _hyena_spectra(seq, hy_w1, hy_b1, hy_w2, hy_b2, hy_w3, hy_freq):
    f32 = F32
    n = 2 * seq
    half = seq // 2
    t = jnp.linspace(0.0, 1.0, seq, dtype=f32)[:, None]
    w = (2.0 * math.pi / seq) * jnp.arange(seq, dtype=f32)[:, None]
    bands = jnp.linspace(1e-4, HY_BANDS - 1, HY_BANDS, dtype=f32)[None, :]
    feats = jnp.concatenate([t, jnp.cos(bands * w), -jnp.sin(bands * w)], axis=-1)
    feats = jnp.pad(feats, ((0, 0), (0, LANES - HY_EMB)))
    pad_f = LANES - HY_FFN
    w1 = jnp.pad(hy_w1.astype(f32), ((0, LANES - HY_EMB), (0, pad_f)))
    b1 = jnp.pad(hy_b1.astype(f32), (0, pad_f))[None, :]
    w2 = jnp.pad(hy_w2.astype(f32), ((0, pad_f), (0, pad_f)))
    b2 = jnp.pad(hy_b2.astype(f32), (0, pad_f))[None, :]
    fr = jnp.pad(hy_freq.astype(f32), (0, pad_f))[None, :]
    w3 = jnp.pad(hy_w3.astype(f32), ((0, pad_f), (0, 0))).reshape(LANES, HY_ORDER, 2, HY_CH)
    w3f = w3[:, :, 0, :].reshape(LANES, HY_ORDER * HY_CH)
    w3b = w3[:, :, 1, :].reshape(LANES, HY_ORDER * HY_CH)
    max_decay = math.log(HY_TARGET) / HY_FAST_PCT
    min_decay = math.log(HY_TARGET) / HY_SLOW_PCT
    deltas = jnp.linspace(min_decay, max_decay, HY_CH, dtype=f32)[None, :]
    sgn = (1.0 - 2.0 * (jnp.arange(seq) % 2).astype(f32))[:, None]
    f_lo = jnp.arange(half, dtype=f32)
    half_w = (math.pi / n) * (jnp.stack([f_lo, seq - 1 - f_lo]) + 0.5)
    cw = jnp.cos(half_w)[:, :, None]
    sw = jnp.sin(half_w)[:, :, None]
    cmat, smat = _dft_matrices(seq)

    tc = HY_CH_TILE
    ncb = HY_CH // tc
    small = lambda shape: pl.BlockSpec(shape, lambda o, c: (0,) * len(shape))
    return pl.pallas_call(
        functools.partial(_filter_kernel, inv_scale=2.0 / n),
        grid=(HY_ORDER, ncb),
        in_specs=[
            small((seq, LANES)), small((LANES, LANES)), small((1, LANES)), small((LANES, LANES)),
            small((1, LANES)), small((1, LANES)),
            pl.BlockSpec((LANES, tc), lambda o, c: (0, o * ncb + c)),
            pl.BlockSpec((LANES, tc), lambda o, c: (0, o * ncb + c)),
            small((seq, 1)),
            pl.BlockSpec((1, tc), lambda o, c: (0, c)),
            small((seq, 1)), small((2, half, 1)), small((2, half, 1)),
            _const_spec((half, seq)), _const_spec((half, seq)),
        ],
        out_specs=[pl.BlockSpec((None, half, tc), lambda o, c: (o, 0, c))] * 4,
        out_shape=[jax.ShapeDtypeStruct((HY_ORDER, half, HY_CH), f32)] * 4,
        scratch_shapes=[pltpu.VMEM((seq, LANES), f32)],
        compiler_params=_params("arbitrary", "arbitrary"),
        name="hyena_filter_spectra",
    )(feats, w1, b1, w2, b2, fr, w3f, w3b, t, deltas, sgn, cw, sw, cmat, smat)


def _cos_sin_outer(row_mult, col_mult, modulus, blk=32):
    rows = row_mult.shape[0]
    step = row_mult[1] - row_mult[0]

    def table(mult):
        ang = ((mult[:, None] * col_mult[None, :]) % modulus).astype(F32) * (2.0 * math.pi / modulus)
        return jnp.cos(ang), jnp.sin(ang)

    ca, sa = (a[:, None, :] for a in table(blk * step * jnp.arange(rows // blk, dtype=jnp.int32)))
    cb, sb = (a[None, :, :] for a in table(row_mult[:blk]))
    shape = (rows, col_mult.shape[0])
    return (ca * cb - sa * sb).reshape(shape), (sa * cb + ca * sb).reshape(shape)


def _dft_matrices(seq):
    odd = 2 * jnp.arange(seq, dtype=jnp.int32) + 1
    cmat, smat = _cos_sin_outer(odd[:seq // 2], odd, 8 * seq)
    return cmat.astype(BF16), smat.astype(BF16)


def _half_dft_tables(seq, width):
    half = seq // 2
    odd_f = 2 * jnp.arange(half, dtype=jnp.int32) + 1
    ce, se = _cos_sin_outer(odd_f, jnp.arange(half, dtype=jnp.int32), 2 * seq)
    tw = []
    for p in range(2):
        ang_p = ((odd_f * (2 * p + 1)) % (8 * seq)).astype(F32) * (2.0 * math.pi / (8 * seq))
        tw += [jnp.cos(ang_p), jnp.sin(ang_p)]
    tw = jnp.broadcast_to(jnp.stack(tw)[:, :, None], (4, half, width))
    return ce.astype(BF16), se.astype(BF16), ce.T.astype(BF16), se.T.astype(BF16), tw


def _hyena_kernel(v_ref, x1_ref, x2_ref, wv_ref, wx1_ref, wx2_ref, bv_ref, bx1_ref, bx2_ref,
                  k1r_ref, k1i_ref, k2r_ref, k2i_ref, skip_ref, tw_ref, ce_ref, se_ref, cet_ref, set_ref,
                  o_ref, stage_ref, z_ref, pre_ref, pim_ref):
    seq = v_ref.shape[0]
    half = seq // 2
    row = lax.broadcasted_iota(jnp.int32, (half, v_ref.shape[1]), 0)
    first = row == 0
    last = row == half - 1
    parities = (pl.ds(0, half, stride=2), pl.ds(1, half, stride=2))
    lane_blocks = [slice(l, l + LANES) for l in range(0, v_ref.shape[1], LANES)]

    def deinterleave(u):
        for j, lanes in enumerate(lane_blocks):
            stage_ref[j] = u[:, lanes]
        return tuple(jnp.concatenate([stage_ref[j, par, :] for j in range(len(lane_blocks))], axis=1)
                     for par in parities)

    def short_conv(u_ref, w_ref, b_ref):
        ev, od = deinterleave(u_ref[...].astype(F32))
        w = w_ref[...]
        b = b_ref[...]
        od_prev = jnp.where(first, 0.0, pltpu.roll(od, 1, 0))
        ev_next = jnp.where(last, 0.0, pltpu.roll(ev, half - 1, 0))
        return (od_prev * w[0:1] + ev * w[1:2] + od * w[2:3] + b,
                ev * w[0:1] + od * w[1:2] + ev_next * w[2:3] + b)

    chunks = [slice(r, r + HY_ROW_CHUNK) for r in range(0, half, HY_ROW_CHUNK)]
    z0 = short_conv(v_ref, wv_ref, bv_ref)
    z_ref[0] = z0[0]
    z_ref[1] = z0[1]
    gate_refs = ((x1_ref, wx1_ref, bx1_ref), (x2_ref, wx2_ref, bx2_ref))
    skip = skip_ref[...]
    for o in range(HY_ORDER):
        zb = (z_ref[0].astype(BF16), z_ref[1].astype(BF16))
        for rows in chunks:
            ce = ce_ref[rows, :]
            se = se_ref[rows, :]
            r_, i_ = [], []
            for p in range(2):
                a = _dot(ce, zb[p])
                b = _dot(se, zb[p])
                c, s = tw_ref[2 * p, rows, :], tw_ref[2 * p + 1, rows, :]
                r_.append(c * a - s * b)
                i_.append(c * b + s * a)
            z1r, z1i = r_[0] + r_[1], -(i_[0] + i_[1])
            z2r, z2i = i_[0] - i_[1], r_[1] - r_[0]
            k1r, k1i = k1r_ref[o, rows, :], k1i_ref[o, rows, :]
            k2r, k2i = k2r_ref[o, rows, :], k2i_ref[o, rows, :]
            y1r, y1i = z1r * k1r - z1i * k1i, z1r * k1i + z1i * k1r
            y2r, y2i = z2r * k2r - z2i * k2i, z2r * k2i + z2i * k2r
            q = ((y1r - y2i, y1i - y2r), (y1r + y2i, y1i + y2r))
            for p in range(2):
                c, s = tw_ref[2 * p, rows, :], tw_ref[2 * p + 1, rows, :]
                pre_ref[p, rows, :] = (c * q[p][0] - s * q[p][1]).astype(BF16)
                pim_ref[p, rows, :] = (-(c * q[p][1] + s * q[p][0])).astype(BF16)
        gate = short_conv(*gate_refs[o])
        for p in range(2):
            pre = pre_ref[p]
            pim = pim_ref[p]
            for rows in chunks:
                y = _dot(cet_ref[rows, :], pre) + _dot(set_ref[rows, :], pim)
                z_ref[p, rows, :] = gate[p][rows] * (y + skip[o:o + 1] * z_ref[p, rows, :])
    for j, lanes in enumerate(lane_blocks):
        stage_ref[j, parities[0], :] = z_ref[0, :, lanes]
        stage_ref[j, parities[1], :] = z_ref[1, :, lanes]
        o_ref[:, lanes] = stage_ref[j].astype(o_ref.dtype)


def _hyena(hy, conv_w, conv_b, spectra, skip, batch, seq):
    tc = HY_CH_TILE
    ncb = HY_CH // tc
    half = seq // 2
    hy3 = hy.reshape(batch, seq, 3 * HY_CH)
    cb = conv_b[None, :]
    ce, se, cet, set_, tw = _half_dft_tables(seq, tc)
    k1r, k1i, k2r, k2i = spectra
    u_spec = lambda g: pl.BlockSpec((None, seq, tc), lambda c, b: (b, 0, g * ncb + c))
    w_spec = lambda g: pl.BlockSpec((3, tc), lambda c, b: (0, g * ncb + c))
    b_spec = lambda g: pl.BlockSpec((1, tc), lambda c, b: (0, g * ncb + c))
    k_spec = pl.BlockSpec((HY_ORDER, half, tc), lambda c, b: (0, 0, c), pipeline_mode=pl.Buffered(1))
    return pl.pallas_call(
        _hyena_kernel,
        grid=(ncb, batch),
        in_specs=[u_spec(0), u_spec(1), u_spec(2), w_spec(0), w_spec(1), w_spec(2),
                  b_spec(0), b_spec(1), b_spec(2), k_spec, k_spec, k_spec, k_spec,
                  pl.BlockSpec((HY_ORDER, tc), lambda c, b: (0, c)),
                  _const_spec((4, half, tc)),
                  _const_spec((half, half)), _const_spec((half, half)),
                  _const_spec((half, half)), _const_spec((half, half))],
        out_specs=pl.BlockSpec((None, seq, tc), lambda c, b: (b, 0, c)),
        out_shape=jax.ShapeDtypeStruct((batch, seq, HY_CH), BF16),
        scratch_shapes=[pltpu.VMEM((tc // LANES, seq, LANES), F32), pltpu.VMEM((2, half, tc), F32),
                        pltpu.VMEM((2, half, tc), BF16), pltpu.VMEM((2, half, tc), BF16)],
        compiler_params=_params("arbitrary", "arbitrary"),
        name="hyena_long_conv",
    )(hy3, hy3, hy3, conv_w, conv_w, conv_w, cb, cb, cb, k1r, k1i, k2r, k2i, skip, tw, ce, se, cet, set_)


def _even_in_kernel(x_ref, g_ref, win_ref, qg_ref, wuq_ref, kvg_ref, wuk_ref, wuv_ref, ones_ref,
                    c_ref, slo_ref, shi_ref, hy_ref, q_ref, k_ref, v_ref):
    c0 = 3 * HY_CH
    c1 = c0 + MLA_Q_RANK
    c2 = c1 + MLA_KV_RANK
    half = MLA_ROPE // 2
    scale = MLA_QK ** -0.5 * LOG2E
    for r in range(0, x_ref.shape[0], ROW_SUBTILE):
        rows = slice(r, r + ROW_SUBTILE)
        xn = _rms(x_ref[rows, :], g_ref[...], EPS).astype(BF16)
        proj = _dot(xn, win_ref[...])
        hy_ref[rows, :] = proj[:, :c0].astype(hy_ref.dtype)
        c = c_ref[rows, :]
        s_lo = slo_ref[rows, :]
        s_hi = shi_ref[rows, :]
        qn = _rms(proj[:, c0:c1], qg_ref[...], EPS).astype(BF16)
        q = _dot(qn, wuq_ref[...])
        kvn = _rms(proj[:, c1:c2], kvg_ref[...], EPS).astype(BF16)
        kn = _dot(kvn, wuk_ref[...])
        v_ref[rows, :] = (_dot(kvn, wuv_ref[...]) + ones_ref[...]).astype(BF16)
        k_pe = _rope(pltpu.roll(proj[:, c2:c2 + LANES], MLA_NOPE, 1), c, s_lo, s_hi, half)
        for h in range(MLA_HEADS):
            blk = slice(h * LANES, (h + 1) * LANES)
            q_ref[rows, blk] = (_rope(q[:, blk], c, s_lo, s_hi, half) * scale).astype(BF16)
            k_ref[rows, blk] = (kn[:, blk] + k_pe).astype(BF16)


def _mla_rope_tables(seq):
    inv = ROPE_THETA ** (-jnp.arange(0, MLA_ROPE, 2, dtype=F32) / MLA_ROPE)
    ang = jnp.arange(seq, dtype=F32)[:, None] * inv[None, :]
    cos, sin = jnp.cos(ang), jnp.sin(ang)
    half = MLA_ROPE // 2
    one = jnp.ones((seq, MLA_NOPE), F32)
    zero = jnp.zeros((seq, MLA_NOPE), F32)
    tail1 = jnp.ones((seq, LANES - MLA_QK), F32)
    tail0 = jnp.zeros((seq, LANES - MLA_QK), F32)
    zh = jnp.zeros((seq, half), F32)
    c = jnp.concatenate([one, cos, cos, tail1], axis=1)
    s_lo = jnp.concatenate([zero, -sin, zh, tail0], axis=1)
    s_hi = jnp.concatenate([zero, zh, sin, tail0], axis=1)
    return c, s_lo, s_hi


def _pad_heads(w, heads, width):
    rows = w.shape[0]
    w = w.reshape(rows, heads, width)
    return jnp.pad(w, ((0, 0), (0, 0), (0, LANES - width))).reshape(rows, heads * LANES)


def _even_in(x2d, g, w_in, q_norm, w_uq, kv_norm, w_ukv, seq):
    rows = x2d.shape[0]
    tm = ROW_TILE
    n_in = w_in.shape[1]
    n_pad = -(-(n_in + LANES - MLA_ROPE) // LANES) * LANES
    win = jnp.pad(w_in, ((0, 0), (0, n_pad - n_in))).astype(BF16)
    wuq = _pad_heads(w_uq, MLA_HEADS, MLA_QK).astype(BF16)
    wkv = w_ukv.reshape(MLA_KV_RANK, MLA_HEADS, MLA_NOPE + MLA_V)
    wuk = _pad_heads(wkv[:, :, :MLA_NOPE].reshape(MLA_KV_RANK, -1), MLA_HEADS, MLA_NOPE).astype(BF16)
    wuv = _pad_heads(wkv[:, :, MLA_NOPE:].reshape(MLA_KV_RANK, -1), MLA_HEADS, MLA_V).astype(BF16)
    c, s_lo, s_hi = _mla_rope_tables(seq)
    tiles_per_seq = seq // tm
    row_spec = lambda width: pl.BlockSpec((tm, width), lambda i: (i, 0))
    tab_spec = pl.BlockSpec((tm, LANES), lambda i: (i % tiles_per_seq, 0))
    hw = MLA_HEADS * LANES
    ones_col = jnp.tile((jnp.arange(LANES) == MLA_V).astype(F32), MLA_HEADS)[None, :]
    return pl.pallas_call(
        _even_in_kernel,
        grid=(rows // tm,),
        in_specs=[row_spec(D_MODEL), _const_spec((1, D_MODEL)), _const_spec(win.shape),
                  _const_spec((1, MLA_Q_RANK)), _const_spec(wuq.shape),
                  _const_spec((1, MLA_KV_RANK)), _const_spec(wuk.shape), _const_spec(wuv.shape),
                  _const_spec((1, hw)), tab_spec, tab_spec, tab_spec],
        out_specs=[row_spec(3 * HY_CH), row_spec(hw), row_spec(hw), row_spec(hw)],
        out_shape=[jax.ShapeDtypeStruct((rows, 3 * HY_CH), BF16)] + [jax.ShapeDtypeStruct((rows, hw), BF16)] * 3,
        compiler_params=_params("arbitrary"),
        name="even_in_proj",
    )(x2d, g[None, :], win, q_norm[None, :], wuq, kv_norm[None, :], wuk, wuv, ones_col, c, s_lo, s_hi)


def _attn_kernel(q_ref, k_ref, v_ref, o_ref, *, heads, sum_lane):
    first_half = lax.broadcasted_iota(jnp.int32, (q_ref.shape[0], LANES), 1) < LANES // 2
    outs = []
    for h in range(heads):
        blk = slice(h * LANES, (h + 1) * LANES)
        s = _dot_nt(q_ref[:, blk], k_ref[:, blk])
        e = jnp.exp2(s - jnp.max(s, axis=-1, keepdims=True)).astype(BF16)
        o = _dot(e, v_ref[:, blk])
        l = o[:, sum_lane:sum_lane + 1]
        outs.append(o / l)
        if h % 2 == 1:
            pair = jnp.where(first_half, outs[h - 1], pltpu.roll(outs[h], LANES // 2, 1))
            o_ref[:, (h // 2) * LANES:(h // 2 + 1) * LANES] = pair.astype(o_ref.dtype)


def _attention(q, k, v, batch, seq, heads, sum_lane):
    tq = MLA_Q_TILE
    hw = heads * LANES
    ow = heads * LANES // 2
    q3, k3, v3 = (a.reshape(batch, seq, hw) for a in (q, k, v))
    q_spec = pl.BlockSpec((None, tq, hw), lambda b, i: (b, i, 0))
    kv_spec = pl.BlockSpec((None, seq, hw), lambda b, i: (b, 0, 0))
    out = pl.pallas_call(
        functools.partial(_attn_kernel, heads=heads, sum_lane=sum_lane),
        grid=(batch, seq // tq),
        in_specs=[q_spec, kv_spec, kv_spec],
        out_specs=pl.BlockSpec((None, tq, ow), lambda b, i: (b, i, 0)),
        out_shape=jax.ShapeDtypeStruct((batch, seq, ow), BF16),
        compiler_params=_params("arbitrary", "arbitrary"),
        name="mla_attention",
    )(q3, k3, v3)
    return out.reshape(batch * seq, ow)


def _odd_in_kernel(x_ref, g_ref, w_ref, c_ref, slo_ref, shi_ref, q_ref, k_ref, v_ref):
    half = DIFF_HD // 2
    scale = DIFF_HD ** -0.5 * LOG2E
    for r in range(0, x_ref.shape[0], ROW_SUBTILE):
        rows = slice(r, r + ROW_SUBTILE)
        xn = _rms(x_ref[rows, :], g_ref[...], EPS).astype(BF16)
        qkv = _dot(xn, w_ref[...])
        c = c_ref[rows, :]
        s_lo = slo_ref[rows, :]
        s_hi = shi_ref[rows, :]
        for h in range(DIFF_HEADS):
            blk = slice(h * LANES, (h + 1) * LANES)
            kblk = slice(D_MODEL + h * LANES, D_MODEL + (h + 1) * LANES)
            q_ref[rows, blk] = (_rope(qkv[:, blk], c, s_lo, s_hi, half) * scale).astype(BF16)
            k_ref[rows, blk] = _rope(qkv[:, kblk], c, s_lo, s_hi, half).astype(BF16)
        v_ref[rows, :] = qkv[:, 2 * D_MODEL:].astype(BF16)


def _diff_rope_tables(seq):
    inv = ROPE_THETA ** (-jnp.arange(0, DIFF_HD, 2, dtype=F32) / DIFF_HD)
    ang = jnp.arange(seq, dtype=F32)[:, None] * inv[None, :]
    cos, sin = jnp.cos(ang), jnp.sin(ang)
    zero = jnp.zeros_like(sin)
    reps = LANES // DIFF_HD
    c = jnp.tile(jnp.concatenate([cos, cos], axis=1), (1, reps))
    s_lo = jnp.tile(jnp.concatenate([-sin, zero], axis=1), (1, reps))
    s_hi = jnp.tile(jnp.concatenate([zero, sin], axis=1), (1, reps))
    return c, s_lo, s_hi


def _odd_in(x2d, g, w_qkv, seq):
    rows = x2d.shape[0]
    tm = ROW_TILE
    c, s_lo, s_hi = _diff_rope_tables(seq)
    tiles_per_seq = seq // tm
    row_spec = lambda width: pl.BlockSpec((tm, width), lambda i: (i, 0))
    tab_spec = pl.BlockSpec((tm, LANES), lambda i: (i % tiles_per_seq, 0))
    w = w_qkv.astype(BF16)
    return pl.pallas_call(
        _odd_in_kernel,
        grid=(rows // tm,),
        in_specs=[row_spec(D_MODEL), _const_spec((1, D_MODEL)), _const_spec(w.shape),
                  tab_spec, tab_spec, tab_spec],
        out_specs=[row_spec(D_MODEL)] * 3,
        out_shape=[jax.ShapeDtypeStruct((rows, D_MODEL), BF16)] * 3,
        compiler_params=_params("arbitrary"),
        name="odd_in_proj",
    )(x2d, g[None, :], w, c, s_lo, s_hi)


def _diff_attn_kernel(lam_ref, q_ref, k_ref, v_ref, g_ref, o_ref, s_ref, e_ref, *, lam_init):
    lp = lam_ref[...]
    lam = (jnp.exp(jnp.sum(lp[0:1] * lp[1:2], axis=-1, keepdims=True))
           - jnp.exp(jnp.sum(lp[2:3] * lp[3:4], axis=-1, keepdims=True)) + lam_init)
    g = g_ref[...] * (1.0 - lam_init)
    tq = q_ref.shape[0]
    first_head = lax.broadcasted_iota(jnp.int32, (tq, LANES), 1) < DIFF_HD
    blocks = [slice(r, r + SOFTMAX_ROWS) for r in range(0, tq, SOFTMAX_ROWS)]

    def scores(h):
        blk = slice(h * LANES, (h + 1) * LANES)
        q = q_ref[:, blk]
        k = k_ref[:, blk]
        zero = jnp.zeros_like(q)
        s_ref[h % 2, 0] = _dot_nt(jnp.where(first_head, q, zero), k)
        s_ref[h % 2, 1] = _dot_nt(jnp.where(first_head, zero, q), k)

    def numerator(slot, j):
        m = jnp.max(s_ref[slot, j], axis=-1, keepdims=True)
        sums = []
        for rows in blocks:
            m_rows = m[rows]
            acc = None
            for c0 in range(0, s_ref.shape[-1], SOFTMAX_KEYS):
                keys = slice(c0, c0 + SOFTMAX_KEYS)
                e = jnp.exp2(s_ref[slot, j, rows, keys] - m_rows)
                for l0 in range(0, SOFTMAX_KEYS, LANES):
                    part = e[:, l0:l0 + LANES]
                    acc = part if acc is None else acc + part
                e_ref[j, rows, keys] = e.astype(BF16)
            sums.append(jnp.sum(acc, axis=-1, keepdims=True))
        return jnp.concatenate(sums, axis=0)

    scores(0)
    for h in range(DIFF_HEADS):
        if h + 1 < DIFF_HEADS:
            scores(h + 1)
        blk = slice(h * LANES, (h + 1) * LANES)
        l1 = numerator(h % 2, 0)
        l2 = numerator(h % 2, 1)
        c = (lam * l1 / l2).astype(BF16)
        for rows in blocks:
            e_ref[0, rows, :] = e_ref[0, rows, :] - c[rows] * e_ref[1, rows, :]
        o = _dot(e_ref[0], v_ref[:, blk]) / l1
        o_ref[:, blk] = (o * lax.rsqrt(jnp.mean(o * o, axis=-1, keepdims=True) + SUBLN_EPS) * g).astype(o_ref.dtype)


def _diff_attention(q, k, v, lam_params, subln, lam_init, batch, seq):
    tq = DIFF_Q_TILE
    q3, k3, v3 = (a.reshape(batch, seq, D_MODEL) for a in (q, k, v))
    q_spec = pl.BlockSpec((None, tq, D_MODEL), lambda b, i: (b, i, 0))
    kv_spec = pl.BlockSpec((None, seq, D_MODEL), lambda b, i: (b, 0, 0))
    lam_tile = jnp.pad(jnp.stack(lam_params).astype(F32), ((0, 4), (0, LANES - DIFF_HD)))
    out = pl.pallas_call(
        functools.partial(_diff_attn_kernel, lam_init=lam_init),
        grid=(batch, seq // tq),
        in_specs=[pl.BlockSpec((8, LANES), lambda b, i: (0, 0)), q_spec, kv_spec, kv_spec,
                  pl.BlockSpec((1, LANES), lambda b, i: (0, 0))],
        out_specs=q_spec,
        out_shape=jax.ShapeDtypeStruct((batch, seq, D_MODEL), BF16),
        scratch_shapes=[pltpu.VMEM((2, 2, tq, seq), F32), pltpu.VMEM((2, tq, seq), BF16)],
        compiler_params=_params("arbitrary", "arbitrary"),
        name="diff_attention",
    )(lam_tile, q3, k3, v3, subln[None, :])
    return out.reshape(batch * seq, D_MODEL)


def _mem_kv_kernel(mem_ref, g_ref, wkv_ref, kt_ref, v_ref):
    nb, n_mem, _ = mem_ref.shape
    mn = _rms(mem_ref[...].reshape(nb * n_mem, D_MODEL), g_ref[...], EPS).astype(BF16)
    for layer in range(wkv_ref.shape[0]):
        kv = _dot(mn, wkv_ref[layer])
        for b in range(nb):
            rows = slice(b * n_mem, (b + 1) * n_mem)
            kt_ref[layer, b] = kv[rows, :D_MODEL].T.astype(BF16)
            v_ref[layer, b] = kv[rows, D_MODEL:].astype(BF16)


def _mem_kv(mem, mem_norm, xa_wkv):
    batch, n_mem, _ = mem.shape
    depth = xa_wkv.shape[0]
    w = xa_wkv.astype(BF16)
    nb = MEM_BATCH_TILE
    return pl.pallas_call(
        _mem_kv_kernel,
        grid=(batch // nb,),
        in_specs=[pl.BlockSpec((nb, n_mem, D_MODEL), lambda b: (b, 0, 0)), _const_spec((1, D_MODEL)),
                  _const_spec(w.shape)],
        out_specs=[pl.BlockSpec((depth, nb, D_MODEL, n_mem), lambda b: (0, b, 0, 0)),
                   pl.BlockSpec((depth, nb, n_mem, D_MODEL), lambda b: (0, b, 0, 0))],
        out_shape=[jax.ShapeDtypeStruct((depth, batch, D_MODEL, n_mem), BF16),
                   jax.ShapeDtypeStruct((depth, batch, n_mem, D_MODEL), BF16)],
        compiler_params=_params("arbitrary"),
        name="memory_kv",
    )(mem, mem_norm[None, :], w)


def _cross_mlp_kernel(*refs, n_mix, final_norm):
    x_ref, o_ref = refs[0], refs[-1]
    mix = refs[1:1 + 2 * n_mix]
    gc_ref, wq_ref, kt_ref, v_ref, wo_ref, gm_ref, wup_ref, wdn_ref, gf_ref = refs[1 + 2 * n_mix:-1]
    x = x_ref[...]
    for a_ref, w_ref in zip(mix[0::2], mix[1::2]):
        x = x + _dot(a_ref[...], w_ref[...])
    hn = _rms(x, gc_ref[...], EPS).astype(BF16)
    q = (_dot(hn, wq_ref[...]) * (X_HD ** -0.5 * LOG2E)).astype(BF16)
    heads = []
    for h in range(X_HEADS):
        blk = slice(h * X_HD, (h + 1) * X_HD)
        s = _dot(q[:, blk], kt_ref[blk, :])
        e = jnp.exp2(s - jnp.max(s, axis=-1, keepdims=True))
        l = jnp.sum(e, axis=-1, keepdims=True)
        heads.append((_dot(e.astype(BF16), v_ref[:, blk]) / l).astype(BF16))
    x = x + _dot(jnp.concatenate(heads, axis=1), wo_ref[...])
    hn = _rms(x, gm_ref[...], EPS).astype(BF16)
    u = jnp.maximum(_dot(hn, wup_ref[...]), 0.0)
    x = x + _dot((u * u).astype(BF16), wdn_ref[...])
    if final_norm:
        x = _rms(x, gf_ref[...], EPS)
    o_ref[...] = x


def _mix_cross_mlp(x2d, mix_pairs, g_cross, wq, kt, v, wo, g_mlp, w_up, w_down, g_final, seq, final_norm):
    rows = x2d.shape[0]
    tm = MLP_ROW_TILE
    tiles_per_seq = seq // tm
    n_mem = v.shape[1]
    row_spec = lambda width: pl.BlockSpec((tm, width), lambda i: (i, 0))
    vec_spec = _const_spec((1, D_MODEL))
    mix_specs, mix_args = [], []
    for a, w in mix_pairs:
        mix_specs += [row_spec(a.shape[1]), _const_spec(w.shape)]
        mix_args += [a, w]
    return pl.pallas_call(
        functools.partial(_cross_mlp_kernel, n_mix=len(mix_pairs), final_norm=final_norm),
        grid=(rows // tm,),
        in_specs=[row_spec(D_MODEL)] + mix_specs + [
            vec_spec, _const_spec((D_MODEL, D_MODEL)),
            pl.BlockSpec((None, D_MODEL, n_mem), lambda i: (i // tiles_per_seq, 0, 0)),
            pl.BlockSpec((None, n_mem, D_MODEL), lambda i: (i // tiles_per_seq, 0, 0)),
            _const_spec((D_MODEL, D_MODEL)), vec_spec, _const_spec((D_MODEL, D_FF)),
            _const_spec((D_FF, D_MODEL)), vec_spec],
        out_specs=row_spec(D_MODEL),
        out_shape=jax.ShapeDtypeStruct((rows, D_MODEL), F32),
        compiler_params=_params("arbitrary"),
        name="mix_cross_mlp",
    )(x2d, *mix_args, g_cross[None, :], wq.astype(BF16), kt, v, wo.astype(BF16), g_mlp[None, :],
      w_up.astype(BF16), w_down.astype(BF16), g_final[None, :])


def kernel(x, mem, ev_w_in, ev_conv_w, ev_conv_b, hy_w1, hy_b1, hy_w2, hy_b2, hy_w3, hy_freq, hy_skip,
           mla_q_norm, mla_w_uq, mla_kv_norm, mla_w_ukv, ev_w_out, od_w_qkv, dif_lq1, dif_lk1, dif_lq2,
           dif_lk2, dif_subln, od_w_out, norm_mix, norm_cross, norm_mlp, xa_wq, xa_wkv, xa_wo, mlp_up,
           mlp_down, mem_norm, final_norm):
    batch, seq, d = x.shape
    depth = norm_mix.shape[0]
    assert d == D_MODEL and all(seq % t == 0 for t in (ROW_TILE, MLP_ROW_TILE, MLA_Q_TILE, DIFF_Q_TILE))
    assert batch % MEM_BATCH_TILE == 0
    x2d = x.reshape(batch * seq, d)
    kt_all, v_all = _mem_kv(mem, mem_norm, xa_wkv)
    for i in range(depth):
        j = i // 2
        if i % 2 == 0:
            spectra = _hyena_spectra(seq, hy_w1[j], hy_b1[j], hy_w2[j], hy_b2[j], hy_w3[j], hy_freq[j])
            hy, q, k, v = _even_in(x2d, norm_mix[i], ev_w_in[j], mla_q_norm[j], mla_w_uq[j],
                                   mla_kv_norm[j], mla_w_ukv[j], seq)
            z = _hyena(hy, ev_conv_w[j], ev_conv_b[j], spectra, hy_skip[j], batch, seq)
            o = _attention(q, k, v, batch, seq, MLA_HEADS, sum_lane=MLA_V)
            w_out = ev_w_out[j].astype(BF16)
            mix = [(z.reshape(batch * seq, HY_CH), w_out[:HY_CH]), (o, w_out[HY_CH:])]
        else:
            lam_init = 0.8 - 0.6 * math.exp(-0.3 * i)
            q, k, v = _odd_in(x2d, norm_mix[i], od_w_qkv[j], seq)
            o = _diff_attention(q, k, v, (dif_lq1[j], dif_lk1[j], dif_lq2[j], dif_lk2[j]), dif_subln[j],
                                lam_init, batch, seq)
            mix = [(o, od_w_out[j].astype(BF16))]
        x2d = _mix_cross_mlp(x2d, mix, norm_cross[i], xa_wq[i], kt_all[i], v_all[i], xa_wo[i], norm_mlp[i],
                             mlp_up[i], mlp_down[i], final_norm, seq, final_norm=(i == depth - 1))
    return x2d.reshape(batch, seq, d)
```

```python
import functools
import math

import jax
import jax.numpy as jnp
from jax import lax
from jax.experimental import pallas as pl
from jax.experimental.pallas import tpu as pltpu

F32 = jnp.float32
BF16 = jnp.bfloat16

D_MODEL = 1024
EPS = 1e-6
ROPE_THETA = 10000.0
HY_CH = 512
HY_ORDER = 2
HY_EMB = 33
HY_BANDS = (HY_EMB - 1) // 2
HY_FFN = 64
HY_FAST_PCT = 0.3
HY_SLOW_PCT = 1.5
HY_TARGET = 1e-2
MLA_HEADS = 8
MLA_NOPE = 64
MLA_ROPE = 32
MLA_V = 64
MLA_QK = MLA_NOPE + MLA_ROPE
MLA_Q_RANK = D_MODEL // 4
MLA_KV_RANK = D_MODEL // 8
DIFF_HEADS = 8
DIFF_HD = D_MODEL // DIFF_HEADS // 2
SUBLN_EPS = 1e-5
X_HEADS = 4
X_HD = D_MODEL // X_HEADS
D_FF = 4 * D_MODEL

LOG2E = math.log2(math.e)
LANES = 128
V7X_VMEM_BYTES = 64 * 1024 * 1024
VMEM_LIMIT = 60000 * 1024
assert VMEM_LIMIT < V7X_VMEM_BYTES

ROW_TILE = 1024
MLP_ROW_TILE = 1024
ROW_SUBTILE = 256
MLA_Q_TILE = 1024
DIFF_Q_TILE = 512
MEM_BATCH_TILE = 4
SOFTMAX_ROWS = 16
SOFTMAX_KEYS = 512
HY_CH_TILE = 256
HY_ROW_CHUNK = 512


def _const_spec(shape):
    nd = len(shape)
    return pl.BlockSpec(shape, lambda *_: (0,) * nd, pipeline_mode=pl.Buffered(1))


def _params(*sem):
    return pltpu.CompilerParams(dimension_semantics=sem, vmem_limit_bytes=VMEM_LIMIT)


def _rms(x, g, eps):
    return x * lax.rsqrt(jnp.mean(x * x, axis=-1, keepdims=True) + eps) * g


def _rope(x, c, s_lo, s_hi, half):
    return x * c + pltpu.roll(x, LANES - half, 1) * s_lo + pltpu.roll(x, half, 1) * s_hi


def _dot(a, b):
    return jnp.dot(a, b, preferred_element_type=F32)


def _dot_nt(a, b):
    return lax.dot_general(a, b, (((1,), (1,)), ((), ())), preferred_element_type=F32)


def _dot_f32(a, b):
    return jnp.dot(a, b, preferred_element_type=F32, precision=lax.Precision.HIGHEST)


def _filter_kernel(feats_ref, w1_ref, b1_ref, w2_ref, b2_ref, fr_ref, w3f_ref, w3b_ref, t_ref, dl_ref, sgn_ref,
                   cw_ref, sw_ref, cm_ref, sm_ref, k1r_ref, k1i_ref, k2r_ref, k2i_ref, act_ref, *, inv_scale):
    @pl.when((pl.program_id(0) == 0) & (pl.program_id(1) == 0))
    def _():
        fr = fr_ref[...]
        a = jnp.sin(fr * (_dot_f32(feats_ref[...], w1_ref[...]) + b1_ref[...]))
        act_ref[...] = jnp.sin(fr * (_dot_f32(a, w2_ref[...]) + b2_ref[...]))

    a = act_ref[...]
    window = jnp.exp(-t_ref[...] * jnp.abs(dl_ref[...]))
    h_f = _dot_f32(a, w3f_ref[...]) * window
    h_b = _dot_f32(a, w3b_ref[...]) * window
    row = lax.broadcasted_iota(jnp.int32, h_b.shape, 0)
    h_b = jnp.where(row == 0, 0.0, h_b)
    h = jnp.concatenate([h_f + h_b, h_f - h_b], axis=1)
    h = jnp.concatenate([h, h * sgn_ref[...]], axis=1).astype(BF16)
    tc = h_f.shape[1]
    col = lambda x, j: x[:, j * tc:(j + 1) * tc]
    for r in range(0, cm_ref.shape[0], HY_ROW_CHUNK):
        rows = slice(r, r + HY_ROW_CHUNK)
        c = _dot(cm_ref[rows, :], h)
        s = _dot(sm_ref[rows, :], h)
        cw, sw = cw_ref[0, rows, :], sw_ref[0, rows, :]
        k1r_ref[rows, :] = (cw * col(c, 0) + sw * col(s, 0)) * inv_scale
        k1i_ref[rows, :] = (sw * col(c, 1) - cw * col(s, 1)) * inv_scale
        cw, sw = cw_ref[1, rows, :], sw_ref[1, rows, :]
        k2r_ref[rows, :] = (cw * col(s, 2) + sw * col(c, 2)) * inv_scale
        k2i_ref[rows, :] = (sw * col(s, 3) - cw * col(c, 3)) * inv_scale


def _hyena_spectra(seq, hy_w1, hy_b1, hy_w2, hy_b2, hy_w3, hy_freq):
    f32 = F32
    n = 2 * seq
    half = seq // 2
    t = jnp.linspace(0.0, 1.0, seq, dtype=f32)[:, None]
    w = (2.0 * math.pi / seq) * jnp.arange(seq, dtype=f32)[:, None]
    bands = jnp.linspace(1e-4, HY_BANDS - 1, HY_BANDS, dtype=f32)[None, :]
    feats = jnp.concatenate([t, jnp.cos(bands * w), -jnp.sin(bands * w)], axis=-1)
    feats = jnp.pad(feats, ((0, 0), (0, LANES - HY_EMB)))
    pad_f = LANES - HY_FFN
    w1 = jnp.pad(hy_w1.astype(f32), ((0, LANES - HY_EMB), (0, pad_f)))
    b1 = jnp.pad(hy_b1.astype(f32), (0, pad_f))[None, :]
    w2 = jnp.pad(hy_w2.astype(f32), ((0, pad_f), (0, pad_f)))
    b2 = jnp.pad(hy_b2.astype(f32), (0, pad_f))[None, :]
    fr = jnp.pad(hy_freq.astype(f32), (0, pad_f))[None, :]
    w3 = jnp.pad(hy_w3.astype(f32), ((0, pad_f), (0, 0))).reshape(LANES, HY_ORDER, 2, HY_CH)
    w3f = w3[:, :, 0, :].reshape(LANES, HY_ORDER * HY_CH)
    w3b = w3[:, :, 1, :].reshape(LANES, HY_ORDER * HY_CH)
    max_decay = math.log(HY_TARGET) / HY_FAST_PCT
    min_decay = math.log(HY_TARGET) / HY_SLOW_PCT
    deltas = jnp.linspace(min_decay, max_decay, HY_CH, dtype=f32)[None, :]
    sgn = (1.0 - 2.0 * (jnp.arange(seq) % 2).astype(f32))[:, None]
    f_lo = jnp.arange(half, dtype=f32)
    half_w = (math.pi / n) * (jnp.stack([f_lo, seq - 1 - f_lo]) + 0.5)
    cw = jnp.cos(half_w)[:, :, None]
    sw = jnp.sin(half_w)[:, :, None]
    cmat, smat = _dft_matrices(seq)

    tc = HY_CH_TILE
    ncb = HY_CH // tc
    small = lambda shape: pl.BlockSpec(shape, lambda o, c: (0,) * len(shape))
    return pl.pallas_call(
        functools.partial(_filter_kernel, inv_scale=2.0 / n),
        grid=(HY_ORDER, ncb),
        in_specs=[
            small((seq, LANES)), small((LANES, LANES)), small((1, LANES)), small((LANES, LANES)),
            small((1, LANES)), small((1, LANES)),
            pl.BlockSpec((LANES, tc), lambda o, c: (0, o * ncb + c)),
            pl.BlockSpec((LANES, tc), lambda o, c: (0, o * ncb + c)),
            small((seq, 1)),
            pl.BlockSpec((1, tc), lambda o, c: (0, c)),
            small((seq, 1)), small((2, half, 1)), small((2, half, 1)),
            _const_spec((half, seq)), _const_spec((half, seq)),
        ],
        out_specs=[pl.BlockSpec((None, half, tc), lambda o, c: (o, 0, c))] * 4,
        out_shape=[jax.ShapeDtypeStruct((HY_ORDER, half, HY_CH), f32)] * 4,
        scratch_shapes=[pltpu.VMEM((seq, LANES), f32)],
        compiler_params=_params("arbitrary", "arbitrary"),
        name="hyena_filter_spectra",
    )(feats, w1, b1, w2, b2, fr, w3f, w3b, t, deltas, sgn, cw, sw, cmat, smat)


def _cos_sin_outer(row_mult, col_mult, modulus, blk=32):
    rows = row_mult.shape[0]
    step = row_mult[1] - row_mult[0]

    def table(mult):
        ang = ((mult[:, None] * col_mult[None, :]) % modulus).astype(F32) * (2.0 * math.pi / modulus)
        return jnp.cos(ang), jnp.sin(ang)

    ca, sa = (a[:, None, :] for a in table(blk * step * jnp.arange(rows // blk, dtype=jnp.int32)))
    cb, sb = (a[None, :, :] for a in table(row_mult[:blk]))
    shape = (rows, col_mult.shape[0])
    return (ca * cb - sa * sb).reshape(shape), (sa * cb + ca * sb).reshape(shape)


def _dft_matrices(seq):
    odd = 2 * jnp.arange(seq, dtype=jnp.int32) + 1
    cmat, smat = _cos_sin_outer(odd[:seq // 2], odd, 8 * seq)
    return cmat.astype(BF16), smat.astype(BF16)


def _half_dft_tables(seq, width):
    half = seq // 2
    odd_f = 2 * jnp.arange(half, dtype=jnp.int32) + 1
    ce, se = _cos_sin_outer(odd_f, jnp.arange(half, dtype=jnp.int32), 2 * seq)
    tw = []
    for p in range(2):
        ang_p = ((odd_f * (2 * p + 1)) % (8 * seq)).astype(F32) * (2.0 * math.pi / (8 * seq))
        tw += [jnp.cos(ang_p), jnp.sin(ang_p)]
    tw = jnp.broadcast_to(jnp.stack(tw)[:, :, None], (4, half, width))
    return ce.astype(BF16), se.astype(BF16), ce.T.astype(BF16), se.T.astype(BF16), tw


def _hyena_kernel(v0_ref, v1_ref, x10_ref, x11_ref, x20_ref, x21_ref, wv_ref, wx1_ref, wx2_ref,
                  bv_ref, bx1_ref, bx2_ref,
                  k1r_ref, k1i_ref, k2r_ref, k2i_ref, skip_ref, tw_ref, ce_ref, se_ref, cet_ref, set_ref,
                  o_ref, stage_ref, z_ref, pre_ref, pim_ref):
    half = v0_ref.shape[0]
    row = lax.broadcasted_iota(jnp.int32, v0_ref.shape, 0)
    first = row == 0
    last = row == half - 1
    parities = (pl.ds(0, half, stride=2), pl.ds(1, half, stride=2))
    lane_blocks = [slice(l, l + LANES) for l in range(0, v0_ref.shape[1], LANES)]

    def short_conv(ev_ref, od_ref, w_ref, b_ref):
        ev = ev_ref[...].astype(F32)
        od = od_ref[...].astype(F32)
        w = w_ref[...]
        b = b_ref[...]
        od_prev = jnp.where(first, 0.0, pltpu.roll(od, 1, 0))
        ev_next = jnp.where(last, 0.0, pltpu.roll(ev, half - 1, 0))
        return (od_prev * w[0:1] + ev * w[1:2] + od * w[2:3] + b,
                ev * w[0:1] + od * w[1:2] + ev_next * w[2:3] + b)

    chunks = [slice(r, r + HY_ROW_CHUNK) for r in range(0, half, HY_ROW_CHUNK)]
    z0 = short_conv(v0_ref, v1_ref, wv_ref, bv_ref)
    z_ref[0] = z0[0]
    z_ref[1] = z0[1]
    gate_refs = ((x10_ref, x11_ref, wx1_ref, bx1_ref), (x20_ref, x21_ref, wx2_ref, bx2_ref))
    skip = skip_ref[...]
    for o in range(HY_ORDER):
        zb = (z_ref[0].astype(BF16), z_ref[1].astype(BF16))
        for rows in chunks:
            ce = ce_ref[rows, :]
            se = se_ref[rows, :]
            r_, i_ = [], []
            for p in range(2):
                a = _dot(ce, zb[p])
                b = _dot(se, zb[p])
                c, s = tw_ref[2 * p, rows, :], tw_ref[2 * p + 1, rows, :]
                r_.append(c * a - s * b)
                i_.append(c * b + s * a)
            z1r, z1i = r_[0] + r_[1], -(i_[0] + i_[1])
            z2r, z2i = i_[0] - i_[1], r_[1] - r_[0]
            k1r, k1i = k1r_ref[o, rows, :], k1i_ref[o, rows, :]
            k2r, k2i = k2r_ref[o, rows, :], k2i_ref[o, rows, :]
            y1r, y1i = z1r * k1r - z1i * k1i, z1r * k1i + z1i * k1r
            y2r, y2i = z2r * k2r - z2i * k2i, z2r * k2i + z2i * k2r
            q = ((y1r - y2i, y1i - y2r), (y1r + y2i, y1i + y2r))
            for p in range(2):
                c, s = tw_ref[2 * p, rows, :], tw_ref[2 * p + 1, rows, :]
                pre_ref[p, rows, :] = (c * q[p][0] - s * q[p][1]).astype(BF16)
                pim_ref[p, rows, :] = (-(c * q[p][1] + s * q[p][0])).astype(BF16)
        gate = short_conv(*gate_refs[o])
        for p in range(2):
            pre = pre_ref[p]
            pim = pim_ref[p]
            for rows in chunks:
                y = _dot(cet_ref[rows, :], pre) + _dot(set_ref[rows, :], pim)
                z_ref[p, rows, :] = gate[p][rows] * (y + skip[o:o + 1] * z_ref[p, rows, :])
    for j, lanes in enumerate(lane_blocks):
        stage_ref[j, parities[0], :] = z_ref[0, :, lanes]
        stage_ref[j, parities[1], :] = z_ref[1, :, lanes]
        o_ref[:, lanes] = stage_ref[j].astype(o_ref.dtype)


def _hyena(hy, conv_w, conv_b, spectra, skip, batch, seq):
    tc = HY_CH_TILE
    ncb = HY_CH // tc
    half = seq // 2
    hy3 = hy.reshape(batch, half, 2 * 3 * HY_CH)
    cb = conv_b[None, :]
    ce, se, cet, set_, tw = _half_dft_tables(seq, tc)
    k1r, k1i, k2r, k2i = spectra
    u_spec = lambda g, p: pl.BlockSpec((None, half, tc), lambda c, b: (b, 0, (3 * p + g) * ncb + c))
    w_spec = lambda g: pl.BlockSpec((3, tc), lambda c, b: (0, g * ncb + c))
    b_spec = lambda g: pl.BlockSpec((1, tc), lambda c, b: (0, g * ncb + c))
    k_spec = pl.BlockSpec((HY_ORDER, half, tc), lambda c, b: (0, 0, c), pipeline_mode=pl.Buffered(1))
    return pl.pallas_call(
        _hyena_kernel,
        grid=(ncb, batch),
        in_specs=[u_spec(0, 0), u_spec(0, 1), u_spec(1, 0), u_spec(1, 1), u_spec(2, 0), u_spec(2, 1),
                  w_spec(0), w_spec(1), w_spec(2),
                  b_spec(0), b_spec(1), b_spec(2), k_spec, k_spec, k_spec, k_spec,
                  pl.BlockSpec((HY_ORDER, tc), lambda c, b: (0, c)),
                  _const_spec((4, half, tc)),
                  _const_spec((half, half)), _const_spec((half, half)),
                  _const_spec((half, half)), _const_spec((half, half))],
        out_specs=pl.BlockSpec((None, seq, tc), lambda c, b: (b, 0, c)),
        out_shape=jax.ShapeDtypeStruct((batch, seq, HY_CH), BF16),
        scratch_shapes=[pltpu.VMEM((tc // LANES, seq, LANES), F32), pltpu.VMEM((2, half, tc), F32),
                        pltpu.VMEM((2, half, tc), BF16), pltpu.VMEM((2, half, tc), BF16)],
        compiler_params=_params("arbitrary", "arbitrary"),
        name="hyena_long_conv",
    )(hy3, hy3, hy3, hy3, hy3, hy3, conv_w, conv_w, conv_w, cb, cb, cb, k1r, k1i, k2r, k2i, skip, tw,
      ce, se, cet, set_)


def _even_in_kernel(x_ref, g_ref, win_ref, qg_ref, wuq_ref, kvg_ref, wuk_ref, wuv_ref, ones_ref,
                    c_ref, slo_ref, shi_ref, hy_ref, q_ref, k_ref, v_ref):
    c0 = 3 * HY_CH
    c1 = c0 + MLA_Q_RANK
    c2 = c1 + MLA_KV_RANK
    half = MLA_ROPE // 2
    scale = MLA_QK ** -0.5 * LOG2E
    for r in range(0, x_ref.shape[0], ROW_SUBTILE):
        rows = slice(r, r + ROW_SUBTILE)
        xn = _rms(x_ref[rows, :], g_ref[...], EPS).astype(BF16)
        proj = _dot(xn, win_ref[...])
        hy_ref[rows, :] = proj[:, :c0].astype(hy_ref.dtype)
        c = c_ref[rows, :]
        s_lo = slo_ref[rows, :]
        s_hi = shi_ref[rows, :]
        qn = _rms(proj[:, c0:c1], qg_ref[...], EPS).astype(BF16)
        q = _dot(qn, wuq_ref[...])
        kvn = _rms(proj[:, c1:c2], kvg_ref[...], EPS).astype(BF16)
        kn = _dot(kvn, wuk_ref[...])
        v_ref[rows, :] = (_dot(kvn, wuv_ref[...]) + ones_ref[...]).astype(BF16)
        k_pe = _rope(pltpu.roll(proj[:, c2:c2 + LANES], MLA_NOPE, 1), c, s_lo, s_hi, half)
        for h in range(MLA_HEADS):
            blk = slice(h * LANES, (h + 1) * LANES)
            q_ref[rows, blk] = (_rope(q[:, blk], c, s_lo, s_hi, half) * scale).astype(BF16)
            k_ref[rows, blk] = (kn[:, blk] + k_pe).astype(BF16)


def _mla_rope_tables(seq):
    inv = ROPE_THETA ** (-jnp.arange(0, MLA_ROPE, 2, dtype=F32) / MLA_ROPE)
    ang = jnp.arange(seq, dtype=F32)[:, None] * inv[None, :]
    cos, sin = jnp.cos(ang), jnp.sin(ang)
    half = MLA_ROPE // 2
    one = jnp.ones((seq, MLA_NOPE), F32)
    zero = jnp.zeros((seq, MLA_NOPE), F32)
    tail1 = jnp.ones((seq, LANES - MLA_QK), F32)
    tail0 = jnp.zeros((seq, LANES - MLA_QK), F32)
    zh = jnp.zeros((seq, half), F32)
    c = jnp.concatenate([one, cos, cos, tail1], axis=1)
    s_lo = jnp.concatenate([zero, -sin, zh, tail0], axis=1)
    s_hi = jnp.concatenate([zero, zh, sin, tail0], axis=1)
    return c, s_lo, s_hi


def _pad_heads(w, heads, width):
    rows = w.shape[0]
    w = w.reshape(rows, heads, width)
    return jnp.pad(w, ((0, 0), (0, 0), (0, LANES - width))).reshape(rows, heads * LANES)


def _even_in(x2d, g, w_in, q_norm, w_uq, kv_norm, w_ukv, seq):
    rows = x2d.shape[0]
    tm = ROW_TILE
    n_in = w_in.shape[1]
    n_pad = -(-(n_in + LANES - MLA_ROPE) // LANES) * LANES
    win = jnp.pad(w_in, ((0, 0), (0, n_pad - n_in))).astype(BF16)
    wuq = _pad_heads(w_uq, MLA_HEADS, MLA_QK).astype(BF16)
    wkv = w_ukv.reshape(MLA_KV_RANK, MLA_HEADS, MLA_NOPE + MLA_V)
    wuk = _pad_heads(wkv[:, :, :MLA_NOPE].reshape(MLA_KV_RANK, -1), MLA_HEADS, MLA_NOPE).astype(BF16)
    wuv = _pad_heads(wkv[:, :, MLA_NOPE:].reshape(MLA_KV_RANK, -1), MLA_HEADS, MLA_V).astype(BF16)
    c, s_lo, s_hi = _mla_rope_tables(seq)
    tiles_per_seq = seq // tm
    row_spec = lambda width: pl.BlockSpec((tm, width), lambda i: (i, 0))
    tab_spec = pl.BlockSpec((tm, LANES), lambda i: (i % tiles_per_seq, 0))
    hw = MLA_HEADS * LANES
    ones_col = jnp.tile((jnp.arange(LANES) == MLA_V).astype(F32), MLA_HEADS)[None, :]
    return pl.pallas_call(
        _even_in_kernel,
        grid=(rows // tm,),
        in_specs=[row_spec(D_MODEL), _const_spec((1, D_MODEL)), _const_spec(win.shape),
                  _const_spec((1, MLA_Q_RANK)), _const_spec(wuq.shape),
                  _const_spec((1, MLA_KV_RANK)), _const_spec(wuk.shape), _const_spec(wuv.shape),
                  _const_spec((1, hw)), tab_spec, tab_spec, tab_spec],
        out_specs=[row_spec(3 * HY_CH), row_spec(hw), row_spec(hw), row_spec(hw)],
        out_shape=[jax.ShapeDtypeStruct((rows, 3 * HY_CH), BF16)] + [jax.ShapeDtypeStruct((rows, hw), BF16)] * 3,
        compiler_params=_params("arbitrary"),
        name="even_in_proj",
    )(x2d, g[None, :], win, q_norm[None, :], wuq, kv_norm[None, :], wuk, wuv, ones_col, c, s_lo, s_hi)


def _attn_kernel(q_ref, k_ref, v_ref, o_ref, *, heads, sum_lane):
    first_half = lax.broadcasted_iota(jnp.int32, (q_ref.shape[0], LANES), 1) < LANES // 2
    outs = []
    for h in range(heads):
        blk = slice(h * LANES, (h + 1) * LANES)
        s = _dot_nt(q_ref[:, blk], k_ref[:, blk])
        e = jnp.exp2(s - jnp.max(s, axis=-1, keepdims=True)).astype(BF16)
        o = _dot(e, v_ref[:, blk])
        l = o[:, sum_lane:sum_lane + 1]
        outs.append(o / l)
        if h % 2 == 1:
            pair = jnp.where(first_half, outs[h - 1], pltpu.roll(outs[h], LANES // 2, 1))
            o_ref[:, (h // 2) * LANES:(h // 2 + 1) * LANES] = pair.astype(o_ref.dtype)


def _attention(q, k, v, batch, seq, heads, sum_lane):
    tq = MLA_Q_TILE
    hw = heads * LANES
    ow = heads * LANES // 2
    q3, k3, v3 = (a.reshape(batch, seq, hw) for a in (q, k, v))
    q_spec = pl.BlockSpec((None, tq, hw), lambda b, i: (b, i, 0))
    kv_spec = pl.BlockSpec((None, seq, hw), lambda b, i: (b, 0, 0))
    out = pl.pallas_call(
        functools.partial(_attn_kernel, heads=heads, sum_lane=sum_lane),
        grid=(batch, seq // tq),
        in_specs=[q_spec, kv_spec, kv_spec],
        out_specs=pl.BlockSpec((None, tq, ow), lambda b, i: (b, i, 0)),
        out_shape=jax.ShapeDtypeStruct((batch, seq, ow), BF16),
        compiler_params=_params("arbitrary", "arbitrary"),
        name="mla_attention",
    )(q3, k3, v3)
    return out.reshape(batch * seq, ow)


def _odd_in_kernel(x_ref, g_ref, w_ref, c_ref, slo_ref, shi_ref, q_ref, k_ref, v_ref):
    half = DIFF_HD // 2
    scale = DIFF_HD ** -0.5 * LOG2E
    for r in range(0, x_ref.shape[0], ROW_SUBTILE):
        rows = slice(r, r + ROW_SUBTILE)
        xn = _rms(x_ref[rows, :], g_ref[...], EPS).astype(BF16)
        qkv = _dot(xn, w_ref[...])
        c = c_ref[rows, :]
        s_lo = slo_ref[rows, :]
        s_hi = shi_ref[rows, :]
        for h in range(DIFF_HEADS):
            blk = slice(h * LANES, (h + 1) * LANES)
            kblk = slice(D_MODEL + h * LANES, D_MODEL + (h + 1) * LANES)
            q_ref[rows, blk] = (_rope(qkv[:, blk], c, s_lo, s_hi, half) * scale).astype(BF16)
            k_ref[rows, blk] = _rope(qkv[:, kblk], c, s_lo, s_hi, half).astype(BF16)
        v_ref[rows, :] = qkv[:, 2 * D_MODEL:].astype(BF16)


def _diff_rope_tables(seq):
    inv = ROPE_THETA ** (-jnp.arange(0, DIFF_HD, 2, dtype=F32) / DIFF_HD)
    ang = jnp.arange(seq, dtype=F32)[:, None] * inv[None, :]
    cos, sin = jnp.cos(ang), jnp.sin(ang)
    zero = jnp.zeros_like(sin)
    reps = LANES // DIFF_HD
    c = jnp.tile(jnp.concatenate([cos, cos], axis=1), (1, reps))
    s_lo = jnp.tile(jnp.concatenate([-sin, zero], axis=1), (1, reps))
    s_hi = jnp.tile(jnp.concatenate([zero, sin], axis=1), (1, reps))
    return c, s_lo, s_hi


def _odd_in(x2d, g, w_qkv, seq):
    rows = x2d.shape[0]
    tm = ROW_TILE
    c, s_lo, s_hi = _diff_rope_tables(seq)
    tiles_per_seq = seq // tm
    row_spec = lambda width: pl.BlockSpec((tm, width), lambda i: (i, 0))
    tab_spec = pl.BlockSpec((tm, LANES), lambda i: (i % tiles_per_seq, 0))
    w = w_qkv.astype(BF16)
    return pl.pallas_call(
        _odd_in_kernel,
        grid=(rows // tm,),
        in_specs=[row_spec(D_MODEL), _const_spec((1, D_MODEL)), _const_spec(w.shape),
                  tab_spec, tab_spec, tab_spec],
        out_specs=[row_spec(D_MODEL)] * 3,
        out_shape=[jax.ShapeDtypeStruct((rows, D_MODEL), BF16)] * 3,
        compiler_params=_params("arbitrary"),
        name="odd_in_proj",
    )(x2d, g[None, :], w, c, s_lo, s_hi)


def _diff_attn_kernel(lam_ref, q_ref, k_ref, v_ref, g_ref, o_ref, s_ref, e_ref, *, lam_init):
    lp = lam_ref[...]
    lam = (jnp.exp(jnp.sum(lp[0:1] * lp[1:2], axis=-1, keepdims=True))
           - jnp.exp(jnp.sum(lp[2:3] * lp[3:4], axis=-1, keepdims=True)) + lam_init)
    g = g_ref[...] * (1.0 - lam_init)
    tq = q_ref.shape[0]
    first_head = lax.broadcasted_iota(jnp.int32, (tq, LANES), 1) < DIFF_HD
    blocks = [slice(r, r + SOFTMAX_ROWS) for r in range(0, tq, SOFTMAX_ROWS)]

    def scores(h):
        blk = slice(h * LANES, (h + 1) * LANES)
        q = q_ref[:, blk]
        k = k_ref[:, blk]
        zero = jnp.zeros_like(q)
        s_ref[h % 2, 0] = _dot_nt(jnp.where(first_head, q, zero), k)
        s_ref[h % 2, 1] = _dot_nt(jnp.where(first_head, zero, q), k)

    def numerator(slot, j):
        m = jnp.max(s_ref[slot, j], axis=-1, keepdims=True)
        sums = []
        for rows in blocks:
            m_rows = m[rows]
            acc = None
            for c0 in range(0, s_ref.shape[-1], SOFTMAX_KEYS):
                keys = slice(c0, c0 + SOFTMAX_KEYS)
                e = jnp.exp2(s_ref[slot, j, rows, keys] - m_rows)
                for l0 in range(0, SOFTMAX_KEYS, LANES):
                    part = e[:, l0:l0 + LANES]
                    acc = part if acc is None else acc + part
                e_ref[j, rows, keys] = e.astype(BF16)
            sums.append(jnp.sum(acc, axis=-1, keepdims=True))
        return jnp.concatenate(sums, axis=0)

    scores(0)
    for h in range(DIFF_HEADS):
        if h + 1 < DIFF_HEADS:
            scores(h + 1)
        blk = slice(h * LANES, (h + 1) * LANES)
        l1 = numerator(h % 2, 0)
        l2 = numerator(h % 2, 1)
        c = (lam * l1 / l2).astype(BF16)
        for rows in blocks:
            e_ref[0, rows, :] = e_ref[0, rows, :] - c[rows] * e_ref[1, rows, :]
        o = _dot(e_ref[0], v_ref[:, blk]) / l1
        o_ref[:, blk] = (o * lax.rsqrt(jnp.mean(o * o, axis=-1, keepdims=True) + SUBLN_EPS) * g).astype(o_ref.dtype)


def _diff_attention(q, k, v, lam_params, subln, lam_init, batch, seq):
    tq = DIFF_Q_TILE
    q3, k3, v3 = (a.reshape(batch, seq, D_MODEL) for a in (q, k, v))
    q_spec = pl.BlockSpec((None, tq, D_MODEL), lambda b, i: (b, i, 0))
    kv_spec = pl.BlockSpec((None, seq, D_MODEL), lambda b, i: (b, 0, 0))
    lam_tile = jnp.pad(jnp.stack(lam_params).astype(F32), ((0, 4), (0, LANES - DIFF_HD)))
    out = pl.pallas_call(
        functools.partial(_diff_attn_kernel, lam_init=lam_init),
        grid=(batch, seq // tq),
        in_specs=[pl.BlockSpec((8, LANES), lambda b, i: (0, 0)), q_spec, kv_spec, kv_spec,
                  pl.BlockSpec((1, LANES), lambda b, i: (0, 0))],
        out_specs=q_spec,
        out_shape=jax.ShapeDtypeStruct((batch, seq, D_MODEL), BF16),
        scratch_shapes=[pltpu.VMEM((2, 2, tq, seq), F32), pltpu.VMEM((2, tq, seq), BF16)],
        compiler_params=_params("arbitrary", "arbitrary"),
        name="diff_attention",
    )(lam_tile, q3, k3, v3, subln[None, :])
    return out.reshape(batch * seq, D_MODEL)


def _mem_kv_kernel(mem_ref, g_ref, wkv_ref, kt_ref, v_ref):
    nb, n_mem, _ = mem_ref.shape
    mn = _rms(mem_ref[...].reshape(nb * n_mem, D_MODEL), g_ref[...], EPS).astype(BF16)
    for layer in range(wkv_ref.shape[0]):
        kv = _dot(mn, wkv_ref[layer])
        for b in range(nb):
            rows = slice(b * n_mem, (b + 1) * n_mem)
            kt_ref[layer, b] = kv[rows, :D_MODEL].T.astype(BF16)
            v_ref[layer, b] = kv[rows, D_MODEL:].astype(BF16)


def _mem_kv(mem, mem_norm, xa_wkv):
    batch, n_mem, _ = mem.shape
    depth = xa_wkv.shape[0]
    w = xa_wkv.astype(BF16)
    nb = MEM_BATCH_TILE
    return pl.pallas_call(
        _mem_kv_kernel,
        grid=(batch // nb,),
        in_specs=[pl.BlockSpec((nb, n_mem, D_MODEL), lambda b: (b, 0, 0)), _const_spec((1, D_MODEL)),
                  _const_spec(w.shape)],
        out_specs=[pl.BlockSpec((depth, nb, D_MODEL, n_mem), lambda b: (0, b, 0, 0)),
                   pl.BlockSpec((depth, nb, n_mem, D_MODEL), lambda b: (0, b, 0, 0))],
        out_shape=[jax.ShapeDtypeStruct((depth, batch, D_MODEL, n_mem), BF16),
                   jax.ShapeDtypeStruct((depth, batch, n_mem, D_MODEL), BF16)],
        compiler_params=_params("arbitrary"),
        name="memory_kv",
    )(mem, mem_norm[None, :], w)


def _cross_mlp_kernel(*refs, n_mix, final_norm):
    x_ref, o_ref = refs[0], refs[-1]
    mix = refs[1:1 + 2 * n_mix]
    gc_ref, wq_ref, kt_ref, v_ref, wo_ref, gm_ref, wup_ref, wdn_ref, gf_ref = refs[1 + 2 * n_mix:-1]
    x = x_ref[...]
    for a_ref, w_ref in zip(mix[0::2], mix[1::2]):
        x = x + _dot(a_ref[...], w_ref[...])
    hn = _rms(x, gc_ref[...], EPS).astype(BF16)
    q = (_dot(hn, wq_ref[...]) * (X_HD ** -0.5 * LOG2E)).astype(BF16)
    heads = []
    for h in range(X_HEADS):
        blk = slice(h * X_HD, (h + 1) * X_HD)
        s = _dot(q[:, blk], kt_ref[blk, :])
        e = jnp.exp2(s - jnp.max(s, axis=-1, keepdims=True))
        l = jnp.sum(e, axis=-1, keepdims=True)
        heads.append((_dot(e.astype(BF16), v_ref[:, blk]) / l).astype(BF16))
    x = x + _dot(jnp.concatenate(heads, axis=1), wo_ref[...])
    hn = _rms(x, gm_ref[...], EPS).astype(BF16)
    u = jnp.maximum(_dot(hn, wup_ref[...]), 0.0)
    x = x + _dot((u * u).astype(BF16), wdn_ref[...])
    if final_norm:
        x = _rms(x, gf_ref[...], EPS)
    o_ref[...] = x


def _mix_cross_mlp(x2d, mix_pairs, g_cross, wq, kt, v, wo, g_mlp, w_up, w_down, g_final, seq, final_norm):
    rows = x2d.shape[0]
    tm = MLP_ROW_TILE
    tiles_per_seq = seq // tm
    n_mem = v.shape[1]
    row_spec = lambda width: pl.BlockSpec((tm, width), lambda i: (i, 0))
    vec_spec = _const_spec((1, D_MODEL))
    mix_specs, mix_args = [], []
    for a, w in mix_pairs:
        mix_specs += [row_spec(a.shape[1]), _const_spec(w.shape)]
        mix_args += [a, w]
    return pl.pallas_call(
        functools.partial(_cross_mlp_kernel, n_mix=len(mix_pairs), final_norm=final_norm),
        grid=(rows // tm,),
        in_specs=[row_spec(D_MODEL)] + mix_specs + [
            vec_spec, _const_spec((D_MODEL, D_MODEL)),
            pl.BlockSpec((None, D_MODEL, n_mem), lambda i: (i // tiles_per_seq, 0, 0)),
            pl.BlockSpec((None, n_mem, D_MODEL), lambda i: (i // tiles_per_seq, 0, 0)),
            _const_spec((D_MODEL, D_MODEL)), vec_spec, _const_spec((D_MODEL, D_FF)),
            _const_spec((D_FF, D_MODEL)), vec_spec],
        out_specs=row_spec(D_MODEL),
        out_shape=jax.ShapeDtypeStruct((rows, D_MODEL), F32),
        compiler_params=_params("arbitrary"),
        name="mix_cross_mlp",
    )(x2d, *mix_args, g_cross[None, :], wq.astype(BF16), kt, v, wo.astype(BF16), g_mlp[None, :],
      w_up.astype(BF16), w_down.astype(BF16), g_final[None, :])


def kernel(x, mem, ev_w_in, ev_conv_w, ev_conv_b, hy_w1, hy_b1, hy_w2, hy_b2, hy_w3, hy_freq, hy_skip,
           mla_q_norm, mla_w_uq, mla_kv_norm, mla_w_ukv, ev_w_out, od_w_qkv, dif_lq1, dif_lk1, dif_lq2,
           dif_lk2, dif_subln, od_w_out, norm_mix, norm_cross, norm_mlp, xa_wq, xa_wkv, xa_wo, mlp_up,
           mlp_down, mem_norm, final_norm):
    batch, seq, d = x.shape
    depth = norm_mix.shape[0]
    assert d == D_MODEL and all(seq % t == 0 for t in (ROW_TILE, MLP_ROW_TILE, MLA_Q_TILE, DIFF_Q_TILE))
    assert batch % MEM_BATCH_TILE == 0
    x2d = x.reshape(batch * seq, d)
    kt_all, v_all = _mem_kv(mem, mem_norm, xa_wkv)
    for i in range(depth):
        j = i // 2
        if i % 2 == 0:
            spectra = _hyena_spectra(seq, hy_w1[j], hy_b1[j], hy_w2[j], hy_b2[j], hy_w3[j], hy_freq[j])
            hy, q, k, v = _even_in(x2d, norm_mix[i], ev_w_in[j], mla_q_norm[j], mla_w_uq[j],
                                   mla_kv_norm[j], mla_w_ukv[j], seq)
            z = _hyena(hy, ev_conv_w[j], ev_conv_b[j], spectra, hy_skip[j], batch, seq)
            o = _attention(q, k, v, batch, seq, MLA_HEADS, sum_lane=MLA_V)
            w_out = ev_w_out[j].astype(BF16)
            mix = [(z.reshape(batch * seq, HY_CH), w_out[:HY_CH]), (o, w_out[HY_CH:])]
        else:
            lam_init = 0.8 - 0.6 * math.exp(-0.3 * i)
            q, k, v = _odd_in(x2d, norm_mix[i], od_w_qkv[j], seq)
            o = _diff_attention(q, k, v, (dif_lq1[j], dif_lk1[j], dif_lq2[j], dif_lk2[j]), dif_subln[j],
                                lam_init, batch, seq)
            mix = [(o, od_w_out[j].astype(BF16))]
        x2d = _mix_cross_mlp(x2d, mix, norm_cross[i], xa_wq[i], kt_all[i], v_all[i], xa_wo[i], norm_mlp[i],
                             mlp_up[i], mlp_down[i], final_norm, seq, final_norm=(i == depth - 1))
    return x2d.reshape(batch, seq, d)
```

```python
import functools
import math

import jax
import jax.numpy as jnp
from jax import lax
from jax.experimental import pallas as pl
from jax.experimental.pallas import tpu as pltpu

F32 = jnp.float32
BF16 = jnp.bfloat16

D_MODEL = 1024
EPS = 1e-6
ROPE_THETA = 10000.0
HY_CH = 512
HY_ORDER = 2
HY_EMB = 33
HY_BANDS = (HY_EMB - 1) // 2
HY_FFN = 64
HY_FAST_PCT = 0.3
HY_SLOW_PCT = 1.5
HY_TARGET = 1e-2
MLA_HEADS = 8
MLA_NOPE = 64
MLA_ROPE = 32
MLA_V = 64
MLA_QK = MLA_NOPE + MLA_ROPE
MLA_Q_RANK = D_MODEL // 4
MLA_KV_RANK = D_MODEL // 8
DIFF_HEADS = 8
DIFF_HD = D_MODEL // DIFF_HEADS // 2
SUBLN_EPS = 1e-5
X_HEADS = 4
X_HD = D_MODEL // X_HEADS
D_FF = 4 * D_MODEL

LOG2E = math.log2(math.e)
LANES = 128
V7X_VMEM_BYTES = 64 * 1024 * 1024
VMEM_LIMIT = 60000 * 1024
assert VMEM_LIMIT < V7X_VMEM_BYTES

ROW_TILE = 1024
MLP_ROW_TILE = 1024
ROW_SUBTILE = 256
MLA_Q_TILE = 1024
DIFF_Q_TILE = 512
MEM_BATCH_TILE = 4
SOFTMAX_ROWS = 16
SOFTMAX_KEYS = 512
HY_CH_TILE = 256
HY_ROW_CHUNK = 512


def _const_spec(shape):
    nd = len(shape)
    return pl.BlockSpec(shape, lambda *_: (0,) * nd, pipeline_mode=pl.Buffered(1))


def _params(*sem):
    return pltpu.CompilerParams(dimension_semantics=sem, vmem_limit_bytes=VMEM_LIMIT)


def _rms(x, g, eps):
    return x * lax.rsqrt(jnp.mean(x * x, axis=-1, keepdims=True) + eps) * g


def _rope(x, c, s_lo, s_hi, half):
    return x * c + pltpu.roll(x, LANES - half, 1) * s_lo + pltpu.roll(x, half, 1) * s_hi


def _dot(a, b):
    return jnp.dot(a, b, preferred_element_type=F32)


def _dot_nt(a, b):
    return lax.dot_general(a, b, (((1,), (1,)), ((), ())), preferred_element_type=F32)


def _dot_f32(a, b):
    return jnp.dot(a, b, preferred_element_type=F32, precision=lax.Precision.HIGHEST)


def _filter_kernel(feats_ref, w1_ref, b1_ref, w2_ref, b2_ref, fr_ref, w3f_ref, w3b_ref, t_ref, dl_ref, sgn_ref,
                   cw_ref, sw_ref, cm_ref, sm_ref, k1r_ref, k1i_ref, k2r_ref, k2i_ref, act_ref, *, inv_scale):
    @pl.when((pl.program_id(0) == 0) & (pl.program_id(1) == 0))
    def _():
        fr = fr_ref[...]
        a = jnp.sin(fr * (_dot_f32(feats_ref[...], w1_ref[...]) + b1_ref[...]))
        act_ref[...] = jnp.sin(fr * (_dot_f32(a, w2_ref[...]) + b2_ref[...]))

    a = act_ref[...]
    window = jnp.exp(-t_ref[...] * jnp.abs(dl_ref[...]))
    h_f = _dot_f32(a, w3f_ref[...]) * window
    h_b = _dot_f32(a, w3b_ref[...]) * window
    row = lax.broadcasted_iota(jnp.int32, h_b.shape, 0)
    h_b = jnp.where(row == 0, 0.0, h_b)
    h = jnp.concatenate([h_f + h_b, h_f - h_b], axis=1)
    h = jnp.concatenate([h, h * sgn_ref[...]], axis=1).astype(BF16)
    tc = h_f.shape[1]
    col = lambda x, j: x[:, j * tc:(j + 1) * tc]
    for r in range(0, cm_ref.shape[0], HY_ROW_CHUNK):
        rows = slice(r, r + HY_ROW_CHUNK)
        c = _dot(cm_ref[rows, :], h)
        s = _dot(sm_ref[rows, :], h)
        cw, sw = cw_ref[0, rows, :], sw_ref[0, rows, :]
        k1r_ref[rows, :] = (cw * col(c, 0) + sw * col(s, 0)) * inv_scale
        k1i_ref[rows, :] = (sw * col(c, 1) - cw * col(s, 1)) * inv_scale
        cw, sw = cw_ref[1, rows, :], sw_ref[1, rows, :]
        k2r_ref[rows, :] = (cw * col(s, 2) + sw * col(c, 2)) * inv_scale
        k2i_ref[rows, :] = (sw * col(s, 3) - cw * col(c, 3)) * inv_scale


def _hyena_spectra(seq, hy_w1, hy_b1, hy_w2, hy_b2, hy_w3, hy_freq):
    f32 = F32
    n = 2 * seq
    half = seq // 2
    t = jnp.linspace(0.0, 1.0, seq, dtype=f32)[:, None]
    w = (2.0 * math.pi / seq) * jnp.arange(seq, dtype=f32)[:, None]
    bands = jnp.linspace(1e-4, HY_BANDS - 1, HY_BANDS, dtype=f32)[None, :]
    feats = jnp.concatenate([t, jnp.cos(bands * w), -jnp.sin(bands * w)], axis=-1)
    feats = jnp.pad(feats, ((0, 0), (0, LANES - HY_EMB)))
    pad_f = LANES - HY_FFN
    w1 = jnp.pad(hy_w1.astype(f32), ((0, LANES - HY_EMB), (0, pad_f)))
    b1 = jnp.pad(hy_b1.astype(f32), (0, pad_f))[None, :]
    w2 = jnp.pad(hy_w2.astype(f32), ((0, pad_f), (0, pad_f)))
    b2 = jnp.pad(hy_b2.astype(f32), (0, pad_f))[None, :]
    fr = jnp.pad(hy_freq.astype(f32), (0, pad_f))[None, :]
    w3 = jnp.pad(hy_w3.astype(f32), ((0, pad_f), (0, 0))).reshape(LANES, HY_ORDER, 2, HY_CH)
    w3f = w3[:, :, 0, :].reshape(LANES, HY_ORDER * HY_CH)
    w3b = w3[:, :, 1, :].reshape(LANES, HY_ORDER * HY_CH)
    max_decay = math.log(HY_TARGET) / HY_FAST_PCT
    min_decay = math.log(HY_TARGET) / HY_SLOW_PCT
    deltas = jnp.linspace(min_decay, max_decay, HY_CH, dtype=f32)[None, :]
    sgn = (1.0 - 2.0 * (jnp.arange(seq) % 2).astype(f32))[:, None]
    f_lo = jnp.arange(half, dtype=f32)
    half_w = (math.pi / n) * (jnp.stack([f_lo, seq - 1 - f_lo]) + 0.5)
    cw = jnp.cos(half_w)[:, :, None]
    sw = jnp.sin(half_w)[:, :, None]
    cmat, smat = _dft_matrices(seq)

    tc = HY_CH_TILE
    ncb = HY_CH // tc
    small = lambda shape: pl.BlockSpec(shape, lambda o, c: (0,) * len(shape))
    return pl.pallas_call(
        functools.partial(_filter_kernel, inv_scale=2.0 / n),
        grid=(HY_ORDER, ncb),
        in_specs=[
            small((seq, LANES)), small((LANES, LANES)), small((1, LANES)), small((LANES, LANES)),
            small((1, LANES)), small((1, LANES)),
            pl.BlockSpec((LANES, tc), lambda o, c: (0, o * ncb + c)),
            pl.BlockSpec((LANES, tc), lambda o, c: (0, o * ncb + c)),
            small((seq, 1)),
            pl.BlockSpec((1, tc), lambda o, c: (0, c)),
            small((seq, 1)), small((2, half, 1)), small((2, half, 1)),
            _const_spec((half, seq)), _const_spec((half, seq)),
        ],
        out_specs=[pl.BlockSpec((None, half, tc), lambda o, c: (o, 0, c))] * 4,
        out_shape=[jax.ShapeDtypeStruct((HY_ORDER, half, HY_CH), f32)] * 4,
        scratch_shapes=[pltpu.VMEM((seq, LANES), f32)],
        compiler_params=_params("arbitrary", "arbitrary"),
        name="hyena_filter_spectra",
    )(feats, w1, b1, w2, b2, fr, w3f, w3b, t, deltas, sgn, cw, sw, cmat, smat)


def _cos_sin_outer(row_mult, col_mult, modulus, blk=32):
    rows = row_mult.shape[0]
    step = row_mult[1] - row_mult[0]

    def table(mult):
        ang = ((mult[:, None] * col_mult[None, :]) % modulus).astype(F32) * (2.0 * math.pi / modulus)
        return jnp.cos(ang), jnp.sin(ang)

    ca, sa = (a[:, None, :] for a in table(blk * step * jnp.arange(rows // blk, dtype=jnp.int32)))
    cb, sb = (a[None, :, :] for a in table(row_mult[:blk]))
    shape = (rows, col_mult.shape[0])
    return (ca * cb - sa * sb).reshape(shape), (sa * cb + ca * sb).reshape(shape)


def _dft_matrices(seq):
    odd = 2 * jnp.arange(seq, dtype=jnp.int32) + 1
    cmat, smat = _cos_sin_outer(odd[:seq // 2], odd, 8 * seq)
    return cmat.astype(BF16), smat.astype(BF16)


def _half_dft_tables(seq, width):
    half = seq // 2
    odd_f = 2 * jnp.arange(half, dtype=jnp.int32) + 1
    ce, se = _cos_sin_outer(odd_f, jnp.arange(half, dtype=jnp.int32), 2 * seq)
    tw = []
    for p in range(2):
        ang_p = ((odd_f * (2 * p + 1)) % (8 * seq)).astype(F32) * (2.0 * math.pi / (8 * seq))
        tw += [jnp.cos(ang_p), jnp.sin(ang_p)]
    tw = jnp.broadcast_to(jnp.stack(tw)[:, :, None], (4, half, width))
    return ce.astype(BF16), se.astype(BF16), ce.T.astype(BF16), se.T.astype(BF16), tw


def _hyena_kernel(*refs, n_cast):
    n_in = 19
    cast_in = refs[n_in:n_in + n_cast]
    cast_out = refs[n_in + n_cast + 1:n_in + 2 * n_cast + 1]
    for src, dst in zip(cast_in, cast_out):
        dst[...] = src[...].astype(dst.dtype)
    _hyena_step(*refs[:n_in], refs[n_in + n_cast], *refs[n_in + 2 * n_cast + 1:])


def _hyena_step(v_ref, x1_ref, x2_ref, wv_ref, wx1_ref, wx2_ref, bv_ref, bx1_ref, bx2_ref,
                k1r_ref, k1i_ref, k2r_ref, k2i_ref, skip_ref, tw_ref, ce_ref, se_ref, cet_ref, set_ref,
                o_ref, stage_ref, z_ref, pre_ref, pim_ref):
    seq = v_ref.shape[0]
    half = seq // 2
    row = lax.broadcasted_iota(jnp.int32, (half, v_ref.shape[1]), 0)
    first = row == 0
    last = row == half - 1
    parities = (pl.ds(0, half, stride=2), pl.ds(1, half, stride=2))
    lane_blocks = [slice(l, l + LANES) for l in range(0, v_ref.shape[1], LANES)]

    def deinterleave(u):
        for j, lanes in enumerate(lane_blocks):
            stage_ref[j] = u[:, lanes]
        return tuple(jnp.concatenate([stage_ref[j, par, :] for j in range(len(lane_blocks))], axis=1)
                     for par in parities)

    def short_conv(u_ref, w_ref, b_ref):
        ev, od = deinterleave(u_ref[...].astype(F32))
        w = w_ref[...]
        b = b_ref[...]
        od_prev = jnp.where(first, 0.0, pltpu.roll(od, 1, 0))
        ev_next = jnp.where(last, 0.0, pltpu.roll(ev, half - 1, 0))
        return (od_prev * w[0:1] + ev * w[1:2] + od * w[2:3] + b,
                ev * w[0:1] + od * w[1:2] + ev_next * w[2:3] + b)

    chunks = [slice(r, r + HY_ROW_CHUNK) for r in range(0, half, HY_ROW_CHUNK)]
    z0 = short_conv(v_ref, wv_ref, bv_ref)
    z_ref[0] = z0[0]
    z_ref[1] = z0[1]
    gate_refs = ((x1_ref, wx1_ref, bx1_ref), (x2_ref, wx2_ref, bx2_ref))
    skip = skip_ref[...]
    for o in range(HY_ORDER):
        zb = (z_ref[0].astype(BF16), z_ref[1].astype(BF16))
        for rows in chunks:
            ce = ce_ref[rows, :]
            se = se_ref[rows, :]
            r_, i_ = [], []
            for p in range(2):
                a = _dot(ce, zb[p])
                b = _dot(se, zb[p])
                c, s = tw_ref[2 * p, rows, :], tw_ref[2 * p + 1, rows, :]
                r_.append(c * a - s * b)
                i_.append(c * b + s * a)
            z1r, z1i = r_[0] + r_[1], -(i_[0] + i_[1])
            z2r, z2i = i_[0] - i_[1], r_[1] - r_[0]
            k1r, k1i = k1r_ref[o, rows, :], k1i_ref[o, rows, :]
            k2r, k2i = k2r_ref[o, rows, :], k2i_ref[o, rows, :]
            y1r, y1i = z1r * k1r - z1i * k1i, z1r * k1i + z1i * k1r
            y2r, y2i = z2r * k2r - z2i * k2i, z2r * k2i + z2i * k2r
            q = ((y1r - y2i, y1i - y2r), (y1r + y2i, y1i + y2r))
            for p in range(2):
                c, s = tw_ref[2 * p, rows, :], tw_ref[2 * p + 1, rows, :]
                pre_ref[p, rows, :] = (c * q[p][0] - s * q[p][1]).astype(BF16)
                pim_ref[p, rows, :] = (-(c * q[p][1] + s * q[p][0])).astype(BF16)
        gate = short_conv(*gate_refs[o])
        for p in range(2):
            pre = pre_ref[p]
            pim = pim_ref[p]
            for rows in chunks:
                y = _dot(cet_ref[rows, :], pre) + _dot(set_ref[rows, :], pim)
                z_ref[p, rows, :] = gate[p][rows] * (y + skip[o:o + 1] * z_ref[p, rows, :])
    for j, lanes in enumerate(lane_blocks):
        stage_ref[j, parities[0], :] = z_ref[0, :, lanes]
        stage_ref[j, parities[1], :] = z_ref[1, :, lanes]
        o_ref[:, lanes] = stage_ref[j].astype(o_ref.dtype)


def _hyena(hy, conv_w, conv_b, spectra, skip, batch, seq, cast=()):
    tc = HY_CH_TILE
    ncb = HY_CH // tc
    half = seq // 2
    hy3 = hy.reshape(batch, seq, 3 * HY_CH)
    cb = conv_b[None, :]
    ce, se, cet, set_, tw = _half_dft_tables(seq, tc)
    k1r, k1i, k2r, k2i = spectra
    u_spec = lambda g: pl.BlockSpec((None, seq, tc), lambda c, b: (b, 0, g * ncb + c))
    w_spec = lambda g: pl.BlockSpec((3, tc), lambda c, b: (0, g * ncb + c))
    b_spec = lambda g: pl.BlockSpec((1, tc), lambda c, b: (0, g * ncb + c))
    k_spec = pl.BlockSpec((HY_ORDER, half, tc), lambda c, b: (0, 0, c), pipeline_mode=pl.Buffered(1))
    steps = ncb * batch
    cast_in, cast_out, cast_shapes = [], [], []
    for w, layer in cast:
        rows_per_step = w.shape[1] // steps
        assert rows_per_step * steps == w.shape[1] and rows_per_step % 16 == 0
        cast_in.append(pl.BlockSpec((None, rows_per_step, w.shape[2]),
                                    lambda c, b, layer=layer: (layer, c * batch + b, 0)))
        cast_out.append(pl.BlockSpec((rows_per_step, w.shape[2]), lambda c, b: (c * batch + b, 0)))
        cast_shapes.append(jax.ShapeDtypeStruct(w.shape[1:], BF16))
    out = pl.pallas_call(
        functools.partial(_hyena_kernel, n_cast=len(cast)),
        grid=(ncb, batch),
        in_specs=[u_spec(0), u_spec(1), u_spec(2), w_spec(0), w_spec(1), w_spec(2),
                  b_spec(0), b_spec(1), b_spec(2), k_spec, k_spec, k_spec, k_spec,
                  pl.BlockSpec((HY_ORDER, tc), lambda c, b: (0, c)),
                  _const_spec((4, half, tc)),
                  _const_spec((half, half)), _const_spec((half, half)),
                  _const_spec((half, half)), _const_spec((half, half))] + cast_in,
        out_specs=[pl.BlockSpec((None, seq, tc), lambda c, b: (b, 0, c))] + cast_out,
        out_shape=[jax.ShapeDtypeStruct((batch, seq, HY_CH), BF16)] + cast_shapes,
        scratch_shapes=[pltpu.VMEM((tc // LANES, seq, LANES), F32), pltpu.VMEM((2, half, tc), F32),
                        pltpu.VMEM((2, half, tc), BF16), pltpu.VMEM((2, half, tc), BF16)],
        compiler_params=_params("arbitrary", "arbitrary"),
        name="hyena_long_conv",
    )(hy3, hy3, hy3, conv_w, conv_w, conv_w, cb, cb, cb, k1r, k1i, k2r, k2i, skip, tw, ce, se, cet, set_,
      *[w for w, _ in cast])
    return out[0], out[1:]


def _even_in_kernel(x_ref, g_ref, win_ref, qg_ref, wuq_ref, kvg_ref, wuk_ref, wuv_ref, ones_ref,
                    c_ref, slo_ref, shi_ref, hy_ref, q_ref, k_ref, v_ref):
    c0 = 3 * HY_CH
    c1 = c0 + MLA_Q_RANK
    c2 = c1 + MLA_KV_RANK
    half = MLA_ROPE // 2
    scale = MLA_QK ** -0.5 * LOG2E
    for r in range(0, x_ref.shape[0], ROW_SUBTILE):
        rows = slice(r, r + ROW_SUBTILE)
        xn = _rms(x_ref[rows, :], g_ref[...], EPS).astype(BF16)
        proj = _dot(xn, win_ref[...])
        hy_ref[rows, :] = proj[:, :c0].astype(hy_ref.dtype)
        c = c_ref[rows, :]
        s_lo = slo_ref[rows, :]
        s_hi = shi_ref[rows, :]
        qn = _rms(proj[:, c0:c1], qg_ref[...], EPS).astype(BF16)
        q = _dot(qn, wuq_ref[...])
        kvn = _rms(proj[:, c1:c2], kvg_ref[...], EPS).astype(BF16)
        kn = _dot(kvn, wuk_ref[...])
        v_ref[rows, :] = (_dot(kvn, wuv_ref[...]) + ones_ref[...]).astype(BF16)
        k_pe = _rope(pltpu.roll(proj[:, c2:c2 + LANES], MLA_NOPE, 1), c, s_lo, s_hi, half)
        for h in range(MLA_HEADS):
            blk = slice(h * LANES, (h + 1) * LANES)
            q_ref[rows, blk] = (_rope(q[:, blk], c, s_lo, s_hi, half) * scale).astype(BF16)
            k_ref[rows, blk] = (kn[:, blk] + k_pe).astype(BF16)


def _mla_rope_tables(seq):
    inv = ROPE_THETA ** (-jnp.arange(0, MLA_ROPE, 2, dtype=F32) / MLA_ROPE)
    ang = jnp.arange(seq, dtype=F32)[:, None] * inv[None, :]
    cos, sin = jnp.cos(ang), jnp.sin(ang)
    half = MLA_ROPE // 2
    one = jnp.ones((seq, MLA_NOPE), F32)
    zero = jnp.zeros((seq, MLA_NOPE), F32)
    tail1 = jnp.ones((seq, LANES - MLA_QK), F32)
    tail0 = jnp.zeros((seq, LANES - MLA_QK), F32)
    zh = jnp.zeros((seq, half), F32)
    c = jnp.concatenate([one, cos, cos, tail1], axis=1)
    s_lo = jnp.concatenate([zero, -sin, zh, tail0], axis=1)
    s_hi = jnp.concatenate([zero, zh, sin, tail0], axis=1)
    return c, s_lo, s_hi


def _pad_heads(w, heads, width):
    rows = w.shape[0]
    w = w.reshape(rows, heads, width)
    return jnp.pad(w, ((0, 0), (0, 0), (0, LANES - width))).reshape(rows, heads * LANES)


def _even_in(x2d, g, w_in, q_norm, w_uq, kv_norm, w_ukv, seq):
    rows = x2d.shape[0]
    tm = ROW_TILE
    n_in = w_in.shape[1]
    n_pad = -(-(n_in + LANES - MLA_ROPE) // LANES) * LANES
    win = jnp.pad(w_in, ((0, 0), (0, n_pad - n_in))).astype(BF16)
    wuq = _pad_heads(w_uq, MLA_HEADS, MLA_QK).astype(BF16)
    wkv = w_ukv.reshape(MLA_KV_RANK, MLA_HEADS, MLA_NOPE + MLA_V)
    wuk = _pad_heads(wkv[:, :, :MLA_NOPE].reshape(MLA_KV_RANK, -1), MLA_HEADS, MLA_NOPE).astype(BF16)
    wuv = _pad_heads(wkv[:, :, MLA_NOPE:].reshape(MLA_KV_RANK, -1), MLA_HEADS, MLA_V).astype(BF16)
    c, s_lo, s_hi = _mla_rope_tables(seq)
    tiles_per_seq = seq // tm
    row_spec = lambda width: pl.BlockSpec((tm, width), lambda i: (i, 0))
    tab_spec = pl.BlockSpec((tm, LANES), lambda i: (i % tiles_per_seq, 0))
    hw = MLA_HEADS * LANES
    ones_col = jnp.tile((jnp.arange(LANES) == MLA_V).astype(F32), MLA_HEADS)[None, :]
    return pl.pallas_call(
        _even_in_kernel,
        grid=(rows // tm,),
        in_specs=[row_spec(D_MODEL), _const_spec((1, D_MODEL)), _const_spec(win.shape),
                  _const_spec((1, MLA_Q_RANK)), _const_spec(wuq.shape),
                  _const_spec((1, MLA_KV_RANK)), _const_spec(wuk.shape), _const_spec(wuv.shape),
                  _const_spec((1, hw)), tab_spec, tab_spec, tab_spec],
        out_specs=[row_spec(3 * HY_CH), row_spec(hw), row_spec(hw), row_spec(hw)],
        out_shape=[jax.ShapeDtypeStruct((rows, 3 * HY_CH), BF16)] + [jax.ShapeDtypeStruct((rows, hw), BF16)] * 3,
        compiler_params=_params("arbitrary"),
        name="even_in_proj",
    )(x2d, g[None, :], win, q_norm[None, :], wuq, kv_norm[None, :], wuk, wuv, ones_col, c, s_lo, s_hi)


def _attn_kernel(q_ref, k_ref, v_ref, o_ref, *, heads, sum_lane):
    first_half = lax.broadcasted_iota(jnp.int32, (q_ref.shape[0], LANES), 1) < LANES // 2
    outs = []
    for h in range(heads):
        blk = slice(h * LANES, (h + 1) * LANES)
        s = _dot_nt(q_ref[:, blk], k_ref[:, blk])
        e = jnp.exp2(s - jnp.max(s, axis=-1, keepdims=True)).astype(BF16)
        o = _dot(e, v_ref[:, blk])
        l = o[:, sum_lane:sum_lane + 1]
        outs.append(o / l)
        if h % 2 == 1:
            pair = jnp.where(first_half, outs[h - 1], pltpu.roll(outs[h], LANES // 2, 1))
            o_ref[:, (h // 2) * LANES:(h // 2 + 1) * LANES] = pair.astype(o_ref.dtype)


def _attention(q, k, v, batch, seq, heads, sum_lane):
    tq = MLA_Q_TILE
    hw = heads * LANES
    ow = heads * LANES // 2
    q3, k3, v3 = (a.reshape(batch, seq, hw) for a in (q, k, v))
    q_spec = pl.BlockSpec((None, tq, hw), lambda b, i: (b, i, 0))
    kv_spec = pl.BlockSpec((None, seq, hw), lambda b, i: (b, 0, 0))
    out = pl.pallas_call(
        functools.partial(_attn_kernel, heads=heads, sum_lane=sum_lane),
        grid=(batch, seq // tq),
        in_specs=[q_spec, kv_spec, kv_spec],
        out_specs=pl.BlockSpec((None, tq, ow), lambda b, i: (b, i, 0)),
        out_shape=jax.ShapeDtypeStruct((batch, seq, ow), BF16),
        compiler_params=_params("arbitrary", "arbitrary"),
        name="mla_attention",
    )(q3, k3, v3)
    return out.reshape(batch * seq, ow)


def _odd_in_kernel(x_ref, g_ref, w_ref, c_ref, slo_ref, shi_ref, q_ref, k_ref, v_ref):
    half = DIFF_HD // 2
    scale = DIFF_HD ** -0.5 * LOG2E
    for r in range(0, x_ref.shape[0], ROW_SUBTILE):
        rows = slice(r, r + ROW_SUBTILE)
        xn = _rms(x_ref[rows, :], g_ref[...], EPS).astype(BF16)
        qkv = _dot(xn, w_ref[...])
        c = c_ref[rows, :]
        s_lo = slo_ref[rows, :]
        s_hi = shi_ref[rows, :]
        for h in range(DIFF_HEADS):
            blk = slice(h * LANES, (h + 1) * LANES)
            kblk = slice(D_MODEL + h * LANES, D_MODEL + (h + 1) * LANES)
            q_ref[rows, blk] = (_rope(qkv[:, blk], c, s_lo, s_hi, half) * scale).astype(BF16)
            k_ref[rows, blk] = _rope(qkv[:, kblk], c, s_lo, s_hi, half).astype(BF16)
        v_ref[rows, :] = qkv[:, 2 * D_MODEL:].astype(BF16)


def _diff_rope_tables(seq):
    inv = ROPE_THETA ** (-jnp.arange(0, DIFF_HD, 2, dtype=F32) / DIFF_HD)
    ang = jnp.arange(seq, dtype=F32)[:, None] * inv[None, :]
    cos, sin = jnp.cos(ang), jnp.sin(ang)
    zero = jnp.zeros_like(sin)
    reps = LANES // DIFF_HD
    c = jnp.tile(jnp.concatenate([cos, cos], axis=1), (1, reps))
    s_lo = jnp.tile(jnp.concatenate([-sin, zero], axis=1), (1, reps))
    s_hi = jnp.tile(jnp.concatenate([zero, sin], axis=1), (1, reps))
    return c, s_lo, s_hi


def _odd_in(x2d, g, w_qkv, seq):
    rows = x2d.shape[0]
    tm = ROW_TILE
    c, s_lo, s_hi = _diff_rope_tables(seq)
    tiles_per_seq = seq // tm
    row_spec = lambda width: pl.BlockSpec((tm, width), lambda i: (i, 0))
    tab_spec = pl.BlockSpec((tm, LANES), lambda i: (i % tiles_per_seq, 0))
    w = w_qkv.astype(BF16)
    return pl.pallas_call(
        _odd_in_kernel,
        grid=(rows // tm,),
        in_specs=[row_spec(D_MODEL), _const_spec((1, D_MODEL)), _const_spec(w.shape),
                  tab_spec, tab_spec, tab_spec],
        out_specs=[row_spec(D_MODEL)] * 3,
        out_shape=[jax.ShapeDtypeStruct((rows, D_MODEL), BF16)] * 3,
        compiler_params=_params("arbitrary"),
        name="odd_in_proj",
    )(x2d, g[None, :], w, c, s_lo, s_hi)


def _diff_attn_kernel(lam_ref, q_ref, k_ref, v_ref, g_ref, o_ref, s_ref, e_ref, *, lam_init):
    lp = lam_ref[...]
    lam = (jnp.exp(jnp.sum(lp[0:1] * lp[1:2], axis=-1, keepdims=True))
           - jnp.exp(jnp.sum(lp[2:3] * lp[3:4], axis=-1, keepdims=True)) + lam_init)
    g = g_ref[...] * (1.0 - lam_init)
    tq = q_ref.shape[0]
    first_head = lax.broadcasted_iota(jnp.int32, (tq, LANES), 1) < DIFF_HD
    blocks = [slice(r, r + SOFTMAX_ROWS) for r in range(0, tq, SOFTMAX_ROWS)]

    def scores(h):
        blk = slice(h * LANES, (h + 1) * LANES)
        q = q_ref[:, blk]
        k = k_ref[:, blk]
        zero = jnp.zeros_like(q)
        s_ref[h % 2, 0] = _dot_nt(jnp.where(first_head, q, zero), k)
        s_ref[h % 2, 1] = _dot_nt(jnp.where(first_head, zero, q), k)

    def numerator(slot, j):
        m = jnp.max(s_ref[slot, j], axis=-1, keepdims=True)
        sums = []
        for rows in blocks:
            m_rows = m[rows]
            acc = None
            for c0 in range(0, s_ref.shape[-1], SOFTMAX_KEYS):
                keys = slice(c0, c0 + SOFTMAX_KEYS)
                e = jnp.exp2(s_ref[slot, j, rows, keys] - m_rows)
                for l0 in range(0, SOFTMAX_KEYS, LANES):
                    part = e[:, l0:l0 + LANES]
                    acc = part if acc is None else acc + part
                e_ref[j, rows, keys] = e.astype(BF16)
            sums.append(jnp.sum(acc, axis=-1, keepdims=True))
        return jnp.concatenate(sums, axis=0)

    scores(0)
    for h in range(DIFF_HEADS):
        if h + 1 < DIFF_HEADS:
            scores(h + 1)
        blk = slice(h * LANES, (h + 1) * LANES)
        l1 = numerator(h % 2, 0)
        l2 = numerator(h % 2, 1)
        c = (lam * l1 / l2).astype(BF16)
        for rows in blocks:
            e_ref[0, rows, :] = e_ref[0, rows, :] - c[rows] * e_ref[1, rows, :]
        o = _dot(e_ref[0], v_ref[:, blk]) / l1
        o_ref[:, blk] = (o * lax.rsqrt(jnp.mean(o * o, axis=-1, keepdims=True) + SUBLN_EPS) * g).astype(o_ref.dtype)


def _diff_attention(q, k, v, lam_params, subln, lam_init, batch, seq):
    tq = DIFF_Q_TILE
    q3, k3, v3 = (a.reshape(batch, seq, D_MODEL) for a in (q, k, v))
    q_spec = pl.BlockSpec((None, tq, D_MODEL), lambda b, i: (b, i, 0))
    kv_spec = pl.BlockSpec((None, seq, D_MODEL), lambda b, i: (b, 0, 0))
    lam_tile = jnp.pad(jnp.stack(lam_params).astype(F32), ((0, 4), (0, LANES - DIFF_HD)))
    out = pl.pallas_call(
        functools.partial(_diff_attn_kernel, lam_init=lam_init),
        grid=(batch, seq // tq),
        in_specs=[pl.BlockSpec((8, LANES), lambda b, i: (0, 0)), q_spec, kv_spec, kv_spec,
                  pl.BlockSpec((1, LANES), lambda b, i: (0, 0))],
        out_specs=q_spec,
        out_shape=jax.ShapeDtypeStruct((batch, seq, D_MODEL), BF16),
        scratch_shapes=[pltpu.VMEM((2, 2, tq, seq), F32), pltpu.VMEM((2, tq, seq), BF16)],
        compiler_params=_params("arbitrary", "arbitrary"),
        name="diff_attention",
    )(lam_tile, q3, k3, v3, subln[None, :])
    return out.reshape(batch * seq, D_MODEL)


def _mem_kv_kernel(mem_ref, g_ref, wkv_ref, kt_ref, v_ref):
    nb, n_mem, _ = mem_ref.shape
    mn = _rms(mem_ref[...].reshape(nb * n_mem, D_MODEL), g_ref[...], EPS).astype(BF16)
    for layer in range(wkv_ref.shape[0]):
        kv = _dot(mn, wkv_ref[layer])
        for b in range(nb):
            rows = slice(b * n_mem, (b + 1) * n_mem)
            kt_ref[layer, b] = kv[rows, :D_MODEL].T.astype(BF16)
            v_ref[layer, b] = kv[rows, D_MODEL:].astype(BF16)


def _mem_kv(mem, mem_norm, xa_wkv):
    batch, n_mem, _ = mem.shape
    depth = xa_wkv.shape[0]
    w = xa_wkv.astype(BF16)
    nb = MEM_BATCH_TILE
    return pl.pallas_call(
        _mem_kv_kernel,
        grid=(batch // nb,),
        in_specs=[pl.BlockSpec((nb, n_mem, D_MODEL), lambda b: (b, 0, 0)), _const_spec((1, D_MODEL)),
                  _const_spec(w.shape)],
        out_specs=[pl.BlockSpec((depth, nb, D_MODEL, n_mem), lambda b: (0, b, 0, 0)),
                   pl.BlockSpec((depth, nb, n_mem, D_MODEL), lambda b: (0, b, 0, 0))],
        out_shape=[jax.ShapeDtypeStruct((depth, batch, D_MODEL, n_mem), BF16),
                   jax.ShapeDtypeStruct((depth, batch, n_mem, D_MODEL), BF16)],
        compiler_params=_params("arbitrary"),
        name="memory_kv",
    )(mem, mem_norm[None, :], w)


def _cross_mlp_kernel(*refs, n_mix, final_norm):
    x_ref, o_ref = refs[0], refs[-1]
    mix = refs[1:1 + 2 * n_mix]
    gc_ref, wq_ref, kt_ref, v_ref, wo_ref, gm_ref, wup_ref, wdn_ref, gf_ref = refs[1 + 2 * n_mix:-1]
    x = x_ref[...]
    for a_ref, w_ref in zip(mix[0::2], mix[1::2]):
        x = x + _dot(a_ref[...], w_ref[...])
    hn = _rms(x, gc_ref[...], EPS).astype(BF16)
    q = (_dot(hn, wq_ref[...]) * (X_HD ** -0.5 * LOG2E)).astype(BF16)
    heads = []
    for h in range(X_HEADS):
        blk = slice(h * X_HD, (h + 1) * X_HD)
        s = _dot(q[:, blk], kt_ref[blk, :])
        e = jnp.exp2(s - jnp.max(s, axis=-1, keepdims=True))
        l = jnp.sum(e, axis=-1, keepdims=True)
        heads.append((_dot(e.astype(BF16), v_ref[:, blk]) / l).astype(BF16))
    x = x + _dot(jnp.concatenate(heads, axis=1), wo_ref[...])
    hn = _rms(x, gm_ref[...], EPS).astype(BF16)
    u = jnp.maximum(_dot(hn, wup_ref[...]), 0.0)
    x = x + _dot((u * u).astype(BF16), wdn_ref[...])
    if final_norm:
        x = _rms(x, gf_ref[...], EPS)
    o_ref[...] = x


def _mix_cross_mlp(x2d, mix_pairs, g_cross, wq, kt, v, wo, g_mlp, w_up, w_down, g_final, seq, final_norm):
    rows = x2d.shape[0]
    tm = MLP_ROW_TILE
    tiles_per_seq = seq // tm
    n_mem = v.shape[1]
    row_spec = lambda width: pl.BlockSpec((tm, width), lambda i: (i, 0))
    vec_spec = _const_spec((1, D_MODEL))
    mix_specs, mix_args = [], []
    for a, w in mix_pairs:
        mix_specs += [row_spec(a.shape[1]), _const_spec(w.shape)]
        mix_args += [a, w]
    return pl.pallas_call(
        functools.partial(_cross_mlp_kernel, n_mix=len(mix_pairs), final_norm=final_norm),
        grid=(rows // tm,),
        in_specs=[row_spec(D_MODEL)] + mix_specs + [
            vec_spec, _const_spec((D_MODEL, D_MODEL)),
            pl.BlockSpec((None, D_MODEL, n_mem), lambda i: (i // tiles_per_seq, 0, 0)),
            pl.BlockSpec((None, n_mem, D_MODEL), lambda i: (i // tiles_per_seq, 0, 0)),
            _const_spec((D_MODEL, D_MODEL)), vec_spec, _const_spec((D_MODEL, D_FF)),
            _const_spec((D_FF, D_MODEL)), vec_spec],
        out_specs=row_spec(D_MODEL),
        out_shape=jax.ShapeDtypeStruct((rows, D_MODEL), F32),
        compiler_params=_params("arbitrary"),
        name="mix_cross_mlp",
    )(x2d, *mix_args, g_cross[None, :], wq.astype(BF16), kt, v, wo.astype(BF16), g_mlp[None, :],
      w_up.astype(BF16), w_down.astype(BF16), g_final[None, :])


def kernel(x, mem, ev_w_in, ev_conv_w, ev_conv_b, hy_w1, hy_b1, hy_w2, hy_b2, hy_w3, hy_freq, hy_skip,
           mla_q_norm, mla_w_uq, mla_kv_norm, mla_w_ukv, ev_w_out, od_w_qkv, dif_lq1, dif_lk1, dif_lq2,
           dif_lk2, dif_subln, od_w_out, norm_mix, norm_cross, norm_mlp, xa_wq, xa_wkv, xa_wo, mlp_up,
           mlp_down, mem_norm, final_norm):
    batch, seq, d = x.shape
    depth = norm_mix.shape[0]
    assert d == D_MODEL and all(seq % t == 0 for t in (ROW_TILE, MLP_ROW_TILE, MLA_Q_TILE, DIFF_Q_TILE))
    assert batch % MEM_BATCH_TILE == 0
    x2d = x.reshape(batch * seq, d)
    kt_all, v_all = _mem_kv(mem, mem_norm, xa_wkv)
    stacked = dict(ev_w_out=ev_w_out, od_w_qkv=od_w_qkv, od_w_out=od_w_out, xa_wq=xa_wq, xa_wo=xa_wo,
                   mlp_up=mlp_up, mlp_down=mlp_down)
    bf16_copies = {}

    def w16(name, index):
        return bf16_copies.get((name, index), stacked[name][index])

    for i in range(depth):
        j = i // 2
        if i % 2 == 0:
            spectra = _hyena_spectra(seq, hy_w1[j], hy_b1[j], hy_w2[j], hy_b2[j], hy_w3[j], hy_freq[j])
            hy, q, k, v = _even_in(x2d, norm_mix[i], ev_w_in[j], mla_q_norm[j], mla_w_uq[j],
                                   mla_kv_norm[j], mla_w_ukv[j], seq)
            wanted = [("ev_w_out", j)]
            for layer in (i, i + 1):
                if layer < depth:
                    wanted += [(n, layer) for n in ("xa_wq", "xa_wo", "mlp_up", "mlp_down")]
            if i + 1 < depth:
                wanted += [("od_w_qkv", j), ("od_w_out", j)]
            z, copies = _hyena(hy, ev_conv_w[j], ev_conv_b[j], spectra, hy_skip[j], batch, seq,
                               cast=[(stacked[n], idx) for n, idx in wanted])
            bf16_copies.update(zip(wanted, copies))
            o = _attention(q, k, v, batch, seq, MLA_HEADS, sum_lane=MLA_V)
            w_out = w16("ev_w_out", j).astype(BF16)
            mix = [(z.reshape(batch * seq, HY_CH), w_out[:HY_CH]), (o, w_out[HY_CH:])]
        else:
            lam_init = 0.8 - 0.6 * math.exp(-0.3 * i)
            q, k, v = _odd_in(x2d, norm_mix[i], w16("od_w_qkv", j), seq)
            o = _diff_attention(q, k, v, (dif_lq1[j], dif_lk1[j], dif_lq2[j], dif_lk2[j]), dif_subln[j],
                                lam_init, batch, seq)
            mix = [(o, w16("od_w_out", j).astype(BF16))]
        x2d = _mix_cross_mlp(x2d, mix, norm_cross[i], w16("xa_wq", i), kt_all[i], v_all[i], w16("xa_wo", i),
                             norm_mlp[i], w16("mlp_up", i), w16("mlp_down", i), final_norm, seq,
                             final_norm=(i == depth - 1))
    return x2d.reshape(batch, seq, d)
```

```python
import functools
import math

import jax
import jax.numpy as jnp
from jax import lax
from jax.experimental import pallas as pl
from jax.experimental.pallas import tpu as pltpu

F32 = jnp.float32
BF16 = jnp.bfloat16

D_MODEL = 1024
EPS = 1e-6
ROPE_THETA = 10000.0
HY_CH = 512
HY_ORDER = 2
HY_EMB = 33
HY_BANDS = (HY_EMB - 1) // 2
HY_FFN = 64
HY_FAST_PCT = 0.3
HY_SLOW_PCT = 1.5
HY_TARGET = 1e-2
MLA_HEADS = 8
MLA_NOPE = 64
MLA_ROPE = 32
MLA_V = 64
MLA_QK = MLA_NOPE + MLA_ROPE
MLA_Q_RANK = D_MODEL // 4
MLA_KV_RANK = D_MODEL // 8
DIFF_HEADS = 8
DIFF_HD = D_MODEL // DIFF_HEADS // 2
SUBLN_EPS = 1e-5
X_HEADS = 4
X_HD = D_MODEL // X_HEADS
D_FF = 4 * D_MODEL

LOG2E = math.log2(math.e)
LANES = 128
V7X_VMEM_BYTES = 64 * 1024 * 1024
VMEM_LIMIT = 60000 * 1024
assert VMEM_LIMIT < V7X_VMEM_BYTES

ROW_TILE = 1024
MLP_ROW_TILE = 1024
ROW_SUBTILE = 256
MLA_Q_TILE = 1024
DIFF_Q_TILE = 512
MEM_BATCH_TILE = 4
SOFTMAX_ROWS = 16
SOFTMAX_KEYS = 512
HY_CH_TILE = 256
HY_ROW_CHUNK = 512


def _const_spec(shape):
    nd = len(shape)
    return pl.BlockSpec(shape, lambda *_: (0,) * nd, pipeline_mode=pl.Buffered(1))


def _params(*sem):
    return pltpu.CompilerParams(dimension_semantics=sem, vmem_limit_bytes=VMEM_LIMIT)


def _rms(x, g, eps):
    return x * lax.rsqrt(jnp.mean(x * x, axis=-1, keepdims=True) + eps) * g


def _rope(x, c, s_lo, s_hi, half):
    return x * c + pltpu.roll(x, LANES - half, 1) * s_lo + pltpu.roll(x, half, 1) * s_hi


def _dot(a, b):
    return jnp.dot(a, b, preferred_element_type=F32)


def _dot_nt(a, b):
    return lax.dot_general(a, b, (((1,), (1,)), ((), ())), preferred_element_type=F32)


def _dot_f32(a, b):
    return jnp.dot(a, b, preferred_element_type=F32, precision=lax.Precision.HIGHEST)


def _filter_kernel(feats_ref, w1_ref, b1_ref, w2_ref, b2_ref, fr_ref, w3f_ref, w3b_ref, t_ref, dl_ref, sgn_ref,
                   cw_ref, sw_ref, cm_ref, sm_ref, k1r_ref, k1i_ref, k2r_ref, k2i_ref, act_ref, *, inv_scale):
    @pl.when((pl.program_id(0) == 0) & (pl.program_id(1) == 0))
    def _():
        fr = fr_ref[...]
        a = jnp.sin(fr * (_dot_f32(feats_ref[...], w1_ref[...]) + b1_ref[...]))
        act_ref[...] = jnp.sin(fr * (_dot_f32(a, w2_ref[...]) + b2_ref[...]))

    a = act_ref[...]
    window = jnp.exp(-t_ref[...] * jnp.abs(dl_ref[...]))
    h_f = _dot_f32(a, w3f_ref[...]) * window
    h_b = _dot_f32(a, w3b_ref[...]) * window
    row = lax.broadcasted_iota(jnp.int32, h_b.shape, 0)
    h_b = jnp.where(row == 0, 0.0, h_b)
    h = jnp.concatenate([h_f + h_b, h_f - h_b], axis=1)
    h = jnp.concatenate([h, h * sgn_ref[...]], axis=1).astype(BF16)
    tc = h_f.shape[1]
    col = lambda x, j: x[:, j * tc:(j + 1) * tc]
    for r in range(0, cm_ref.shape[0], HY_ROW_CHUNK):
        rows = slice(r, r + HY_ROW_CHUNK)
        c = _dot(cm_ref[rows, :], h)
        s = _dot(sm_ref[rows, :], h)
        cw, sw = cw_ref[0, rows, :], sw_ref[0, rows, :]
        k1r_ref[rows, :] = (cw * col(c, 0) + sw * col(s, 0)) * inv_scale
        k1i_ref[rows, :] = (sw * col(c, 1) - cw * col(s, 1)) * inv_scale
        cw, sw = cw_ref[1, rows, :], sw_ref[1, rows, :]
        k2r_ref[rows, :] = (cw * col(s, 2) + sw * col(c, 2)) * inv_scale
        k2i_ref[rows, :] = (sw * col(s, 3) - cw * col(c, 3)) * inv_scale


def _hyena_spectra(seq, hy_w1, hy_b1, hy_w2, hy_b2, hy_w3, hy_freq):
    f32 = F32
    n = 2 * seq
    half = seq // 2
    t = jnp.linspace(0.0, 1.0, seq, dtype=f32)[:, None]
    w = (2.0 * math.pi / seq) * jnp.arange(seq, dtype=f32)[:, None]
    bands = jnp.linspace(1e-4, HY_BANDS - 1, HY_BANDS, dtype=f32)[None, :]
    feats = jnp.concatenate([t, jnp.cos(bands * w), -jnp.sin(bands * w)], axis=-1)
    feats = jnp.pad(feats, ((0, 0), (0, LANES - HY_EMB)))
    pad_f = LANES - HY_FFN
    w1 = jnp.pad(hy_w1.astype(f32), ((0, LANES - HY_EMB), (0, pad_f)))
    b1 = jnp.pad(hy_b1.astype(f32), (0, pad_f))[None, :]
    w2 = jnp.pad(hy_w2.astype(f32), ((0, pad_f), (0, pad_f)))
    b2 = jnp.pad(hy_b2.astype(f32), (0, pad_f))[None, :]
    fr = jnp.pad(hy_freq.astype(f32), (0, pad_f))[None, :]
    w3 = jnp.pad(hy_w3.astype(f32), ((0, pad_f), (0, 0))).reshape(LANES, HY_ORDER, 2, HY_CH)
    w3f = w3[:, :, 0, :].reshape(LANES, HY_ORDER * HY_CH)
    w3b = w3[:, :, 1, :].reshape(LANES, HY_ORDER * HY_CH)
    max_decay = math.log(HY_TARGET) / HY_FAST_PCT
    min_decay = math.log(HY_TARGET) / HY_SLOW_PCT
    deltas = jnp.linspace(min_decay, max_decay, HY_CH, dtype=f32)[None, :]
    sgn = (1.0 - 2.0 * (jnp.arange(seq) % 2).astype(f32))[:, None]
    f_lo = jnp.arange(half, dtype=f32)
    half_w = (math.pi / n) * (jnp.stack([f_lo, seq - 1 - f_lo]) + 0.5)
    cw = jnp.cos(half_w)[:, :, None]
    sw = jnp.sin(half_w)[:, :, None]
    cmat, smat = _dft_matrices(seq)

    tc = HY_CH_TILE
    ncb = HY_CH // tc
    small = lambda shape: pl.BlockSpec(shape, lambda o, c: (0,) * len(shape))
    return pl.pallas_call(
        functools.partial(_filter_kernel, inv_scale=2.0 / n),
        grid=(HY_ORDER, ncb),
        in_specs=[
            small((seq, LANES)), small((LANES, LANES)), small((1, LANES)), small((LANES, LANES)),
            small((1, LANES)), small((1, LANES)),
            pl.BlockSpec((LANES, tc), lambda o, c: (0, o * ncb + c)),
            pl.BlockSpec((LANES, tc), lambda o, c: (0, o * ncb + c)),
            small((seq, 1)),
            pl.BlockSpec((1, tc), lambda o, c: (0, c)),
            small((seq, 1)), small((2, half, 1)), small((2, half, 1)),
            _const_spec((half, seq)), _const_spec((half, seq)),
        ],
        out_specs=[pl.BlockSpec((None, half, tc), lambda o, c: (o, 0, c))] * 4,
        out_shape=[jax.ShapeDtypeStruct((HY_ORDER, half, HY_CH), f32)] * 4,
        scratch_shapes=[pltpu.VMEM((seq, LANES), f32)],
        compiler_params=_params("arbitrary", "arbitrary"),
        name="hyena_filter_spectra",
    )(feats, w1, b1, w2, b2, fr, w3f, w3b, t, deltas, sgn, cw, sw, cmat, smat)


def _cos_sin_outer(row_mult, col_mult, modulus, blk=32):
    rows = row_mult.shape[0]
    step = row_mult[1] - row_mult[0]

    def table(mult):
        ang = ((mult[:, None] * col_mult[None, :]) % modulus).astype(F32) * (2.0 * math.pi / modulus)
        return jnp.cos(ang), jnp.sin(ang)

    ca, sa = (a[:, None, :] for a in table(blk * step * jnp.arange(rows // blk, dtype=jnp.int32)))
    cb, sb = (a[None, :, :] for a in table(row_mult[:blk]))
    shape = (rows, col_mult.shape[0])
    return (ca * cb - sa * sb).reshape(shape), (sa * cb + ca * sb).reshape(shape)


def _dft_matrices(seq):
    odd = 2 * jnp.arange(seq, dtype=jnp.int32) + 1
    cmat, smat = _cos_sin_outer(odd[:seq // 2], odd, 8 * seq)
    return cmat.astype(BF16), smat.astype(BF16)


def _half_dft_tables(seq, width):
    half = seq // 2
    odd_f = 2 * jnp.arange(half, dtype=jnp.int32) + 1
    ce, se = _cos_sin_outer(odd_f, jnp.arange(half, dtype=jnp.int32), 2 * seq)
    tw = []
    for p in range(2):
        ang_p = ((odd_f * (2 * p + 1)) % (8 * seq)).astype(F32) * (2.0 * math.pi / (8 * seq))
        tw += [jnp.cos(ang_p), jnp.sin(ang_p)]
    tw = jnp.broadcast_to(jnp.stack(tw)[:, :, None], (4, half, width))
    return ce.astype(BF16), se.astype(BF16), ce.T.astype(BF16), se.T.astype(BF16), tw


def _hyena_kernel(*refs, n_cast):
    n_in = 19
    cast_in = refs[n_in:n_in + n_cast]
    cast_out = refs[n_in + n_cast + 1:n_in + 2 * n_cast + 1]
    for src, dst in zip(cast_in, cast_out):
        dst[...] = src[...].astype(dst.dtype)
    _hyena_step(*refs[:n_in], refs[n_in + n_cast], *refs[n_in + 2 * n_cast + 1:])


def _hyena_step(v_ref, x1_ref, x2_ref, wv_ref, wx1_ref, wx2_ref, bv_ref, bx1_ref, bx2_ref,
                k1r_ref, k1i_ref, k2r_ref, k2i_ref, skip_ref, tw_ref, ce_ref, se_ref, cet_ref, set_ref,
                o_ref, stage_ref, z_ref, pre_ref, pim_ref):
    seq = v_ref.shape[0]
    half = seq // 2
    row = lax.broadcasted_iota(jnp.int32, (half, v_ref.shape[1]), 0)
    first = row == 0
    last = row == half - 1
    parities = (pl.ds(0, half, stride=2), pl.ds(1, half, stride=2))
    lane_blocks = [slice(l, l + LANES) for l in range(0, v_ref.shape[1], LANES)]

    def deinterleave(u):
        for j, lanes in enumerate(lane_blocks):
            stage_ref[j] = u[:, lanes]
        return tuple(jnp.concatenate([stage_ref[j, par, :] for j in range(len(lane_blocks))], axis=1)
                     for par in parities)

    def short_conv(u_ref, w_ref, b_ref):
        ev, od = deinterleave(u_ref[...].astype(F32))
        w = w_ref[...]
        b = b_ref[...]
        od_prev = jnp.where(first, 0.0, pltpu.roll(od, 1, 0))
        ev_next = jnp.where(last, 0.0, pltpu.roll(ev, half - 1, 0))
        return (od_prev * w[0:1] + ev * w[1:2] + od * w[2:3] + b,
                ev * w[0:1] + od * w[1:2] + ev_next * w[2:3] + b)

    chunks = [slice(r, r + HY_ROW_CHUNK) for r in range(0, half, HY_ROW_CHUNK)]
    z0 = short_conv(v_ref, wv_ref, bv_ref)
    z_ref[0] = z0[0]
    z_ref[1] = z0[1]
    gate_refs = ((x1_ref, wx1_ref, bx1_ref), (x2_ref, wx2_ref, bx2_ref))
    skip = skip_ref[...]
    for o in range(HY_ORDER):
        zb = (z_ref[0].astype(BF16), z_ref[1].astype(BF16))
        for rows in chunks:
            ce = ce_ref[rows, :]
            se = se_ref[rows, :]
            r_, i_ = [], []
            for p in range(2):
                a = _dot(ce, zb[p])
                b = _dot(se, zb[p])
                c, s = tw_ref[2 * p, rows, :], tw_ref[2 * p + 1, rows, :]
                r_.append(c * a - s * b)
                i_.append(c * b + s * a)
            z1r, z1i = r_[0] + r_[1], -(i_[0] + i_[1])
            z2r, z2i = i_[0] - i_[1], r_[1] - r_[0]
            k1r, k1i = k1r_ref[o, rows, :], k1i_ref[o, rows, :]
            k2r, k2i = k2r_ref[o, rows, :], k2i_ref[o, rows, :]
            y1r, y1i = z1r * k1r - z1i * k1i, z1r * k1i + z1i * k1r
            y2r, y2i = z2r * k2r - z2i * k2i, z2r * k2i + z2i * k2r
            q = ((y1r - y2i, y1i - y2r), (y1r + y2i, y1i + y2r))
            for p in range(2):
                c, s = tw_ref[2 * p, rows, :], tw_ref[2 * p + 1, rows, :]
                pre_ref[p, rows, :] = (c * q[p][0] - s * q[p][1]).astype(BF16)
                pim_ref[p, rows, :] = (-(c * q[p][1] + s * q[p][0])).astype(BF16)
        gate = short_conv(*gate_refs[o])
        for p in range(2):
            pre = pre_ref[p]
            pim = pim_ref[p]
            for rows in chunks:
                y = _dot(cet_ref[rows, :], pre) + _dot(set_ref[rows, :], pim)
                z_ref[p, rows, :] = gate[p][rows] * (y + skip[o:o + 1] * z_ref[p, rows, :])
    for j, lanes in enumerate(lane_blocks):
        stage_ref[j, parities[0], :] = z_ref[0, :, lanes]
        stage_ref[j, parities[1], :] = z_ref[1, :, lanes]
        o_ref[:, lanes] = stage_ref[j].astype(o_ref.dtype)


def _hyena(hy, conv_w, conv_b, spectra, skip, batch, seq, cast=()):
    tc = HY_CH_TILE
    ncb = HY_CH // tc
    half = seq // 2
    hy3 = hy.reshape(batch, seq, 3 * HY_CH)
    cb = conv_b[None, :]
    ce, se, cet, set_, tw = _half_dft_tables(seq, tc)
    k1r, k1i, k2r, k2i = spectra
    u_spec = lambda g: pl.BlockSpec((None, seq, tc), lambda c, b: (b, 0, g * ncb + c))
    w_spec = lambda g: pl.BlockSpec((3, tc), lambda c, b: (0, g * ncb + c))
    b_spec = lambda g: pl.BlockSpec((1, tc), lambda c, b: (0, g * ncb + c))
    k_spec = pl.BlockSpec((HY_ORDER, half, tc), lambda c, b: (0, 0, c), pipeline_mode=pl.Buffered(1))
    steps = ncb * batch
    cast_in, cast_out, cast_shapes = [], [], []
    for w, layer, row0, n_rows in cast:
        rows_per_step = n_rows // steps
        assert rows_per_step * steps == n_rows and rows_per_step % 16 == 0 and row0 % rows_per_step == 0
        cast_in.append(pl.BlockSpec(
            (None, rows_per_step, w.shape[2]),
            lambda c, b, layer=layer, first=row0 // rows_per_step: (layer, first + c * batch + b, 0)))
        cast_out.append(pl.BlockSpec((rows_per_step, w.shape[2]), lambda c, b: (c * batch + b, 0)))
        cast_shapes.append(jax.ShapeDtypeStruct((n_rows, w.shape[2]), BF16))
    out = pl.pallas_call(
        functools.partial(_hyena_kernel, n_cast=len(cast)),
        grid=(ncb, batch),
        in_specs=[u_spec(0), u_spec(1), u_spec(2), w_spec(0), w_spec(1), w_spec(2),
                  b_spec(0), b_spec(1), b_spec(2), k_spec, k_spec, k_spec, k_spec,
                  pl.BlockSpec((HY_ORDER, tc), lambda c, b: (0, c)),
                  _const_spec((4, half, tc)),
                  _const_spec((half, half)), _const_spec((half, half)),
                  _const_spec((half, half)), _const_spec((half, half))] + cast_in,
        out_specs=[pl.BlockSpec((None, seq, tc), lambda c, b: (b, 0, c))] + cast_out,
        out_shape=[jax.ShapeDtypeStruct((batch, seq, HY_CH), BF16)] + cast_shapes,
        scratch_shapes=[pltpu.VMEM((tc // LANES, seq, LANES), F32), pltpu.VMEM((2, half, tc), F32),
                        pltpu.VMEM((2, half, tc), BF16), pltpu.VMEM((2, half, tc), BF16)],
        compiler_params=_params("arbitrary", "arbitrary"),
        name="hyena_long_conv",
    )(hy3, hy3, hy3, conv_w, conv_w, conv_w, cb, cb, cb, k1r, k1i, k2r, k2i, skip, tw, ce, se, cet, set_,
      *[item[0] for item in cast])
    return out[0], out[1:]


def _even_in_kernel(x_ref, g_ref, win_ref, qg_ref, wuq_ref, kvg_ref, wuk_ref, wuv_ref, ones_ref,
                    c_ref, slo_ref, shi_ref, hy_ref, q_ref, k_ref, v_ref):
    c0 = 3 * HY_CH
    c1 = c0 + MLA_Q_RANK
    c2 = c1 + MLA_KV_RANK
    half = MLA_ROPE // 2
    scale = MLA_QK ** -0.5 * LOG2E
    for r in range(0, x_ref.shape[0], ROW_SUBTILE):
        rows = slice(r, r + ROW_SUBTILE)
        xn = _rms(x_ref[rows, :], g_ref[...], EPS).astype(BF16)
        proj = _dot(xn, win_ref[...])
        hy_ref[rows, :] = proj[:, :c0].astype(hy_ref.dtype)
        c = c_ref[rows, :]
        s_lo = slo_ref[rows, :]
        s_hi = shi_ref[rows, :]
        qn = _rms(proj[:, c0:c1], qg_ref[...], EPS).astype(BF16)
        q = _dot(qn, wuq_ref[...])
        kvn = _rms(proj[:, c1:c2], kvg_ref[...], EPS).astype(BF16)
        kn = _dot(kvn, wuk_ref[...])
        v_ref[rows, :] = (_dot(kvn, wuv_ref[...]) + ones_ref[...]).astype(BF16)
        k_pe = _rope(pltpu.roll(proj[:, c2:c2 + LANES], MLA_NOPE, 1), c, s_lo, s_hi, half)
        for h in range(MLA_HEADS):
            blk = slice(h * LANES, (h + 1) * LANES)
            q_ref[rows, blk] = (_rope(q[:, blk], c, s_lo, s_hi, half) * scale).astype(BF16)
            k_ref[rows, blk] = (kn[:, blk] + k_pe).astype(BF16)


def _mla_rope_tables(seq):
    inv = ROPE_THETA ** (-jnp.arange(0, MLA_ROPE, 2, dtype=F32) / MLA_ROPE)
    ang = jnp.arange(seq, dtype=F32)[:, None] * inv[None, :]
    cos, sin = jnp.cos(ang), jnp.sin(ang)
    half = MLA_ROPE // 2
    one = jnp.ones((seq, MLA_NOPE), F32)
    zero = jnp.zeros((seq, MLA_NOPE), F32)
    tail1 = jnp.ones((seq, LANES - MLA_QK), F32)
    tail0 = jnp.zeros((seq, LANES - MLA_QK), F32)
    zh = jnp.zeros((seq, half), F32)
    c = jnp.concatenate([one, cos, cos, tail1], axis=1)
    s_lo = jnp.concatenate([zero, -sin, zh, tail0], axis=1)
    s_hi = jnp.concatenate([zero, zh, sin, tail0], axis=1)
    return c, s_lo, s_hi


def _pad_heads(w, heads, width):
    rows = w.shape[0]
    w = w.reshape(rows, heads, width)
    return jnp.pad(w, ((0, 0), (0, 0), (0, LANES - width))).reshape(rows, heads * LANES)


def _even_in(x2d, g, w_in, q_norm, w_uq, kv_norm, w_ukv, seq):
    rows = x2d.shape[0]
    tm = ROW_TILE
    n_in = w_in.shape[1]
    n_pad = -(-(n_in + LANES - MLA_ROPE) // LANES) * LANES
    win = jnp.pad(w_in, ((0, 0), (0, n_pad - n_in))).astype(BF16)
    wuq = _pad_heads(w_uq, MLA_HEADS, MLA_QK).astype(BF16)
    wkv = w_ukv.reshape(MLA_KV_RANK, MLA_HEADS, MLA_NOPE + MLA_V)
    wuk = _pad_heads(wkv[:, :, :MLA_NOPE].reshape(MLA_KV_RANK, -1), MLA_HEADS, MLA_NOPE).astype(BF16)
    wuv = _pad_heads(wkv[:, :, MLA_NOPE:].reshape(MLA_KV_RANK, -1), MLA_HEADS, MLA_V).astype(BF16)
    c, s_lo, s_hi = _mla_rope_tables(seq)
    tiles_per_seq = seq // tm
    row_spec = lambda width: pl.BlockSpec((tm, width), lambda i: (i, 0))
    tab_spec = pl.BlockSpec((tm, LANES), lambda i: (i % tiles_per_seq, 0))
    hw = MLA_HEADS * LANES
    ones_col = jnp.tile((jnp.arange(LANES) == MLA_V).astype(F32), MLA_HEADS)[None, :]
    return pl.pallas_call(
        _even_in_kernel,
        grid=(rows // tm,),
        in_specs=[row_spec(D_MODEL), _const_spec((1, D_MODEL)), _const_spec(win.shape),
                  _const_spec((1, MLA_Q_RANK)), _const_spec(wuq.shape),
                  _const_spec((1, MLA_KV_RANK)), _const_spec(wuk.shape), _const_spec(wuv.shape),
                  _const_spec((1, hw)), tab_spec, tab_spec, tab_spec],
        out_specs=[row_spec(3 * HY_CH), row_spec(hw), row_spec(hw), row_spec(hw)],
        out_shape=[jax.ShapeDtypeStruct((rows, 3 * HY_CH), BF16)] + [jax.ShapeDtypeStruct((rows, hw), BF16)] * 3,
        compiler_params=_params("arbitrary"),
        name="even_in_proj",
    )(x2d, g[None, :], win, q_norm[None, :], wuq, kv_norm[None, :], wuk, wuv, ones_col, c, s_lo, s_hi)


def _attn_kernel(q_ref, k_ref, v_ref, o_ref, *, heads, sum_lane):
    first_half = lax.broadcasted_iota(jnp.int32, (q_ref.shape[0], LANES), 1) < LANES // 2
    outs = []
    for h in range(heads):
        blk = slice(h * LANES, (h + 1) * LANES)
        s = _dot_nt(q_ref[:, blk], k_ref[:, blk])
        e = jnp.exp2(s - jnp.max(s, axis=-1, keepdims=True)).astype(BF16)
        o = _dot(e, v_ref[:, blk])
        l = o[:, sum_lane:sum_lane + 1]
        outs.append(o / l)
        if h % 2 == 1:
            pair = jnp.where(first_half, outs[h - 1], pltpu.roll(outs[h], LANES // 2, 1))
            o_ref[:, (h // 2) * LANES:(h // 2 + 1) * LANES] = pair.astype(o_ref.dtype)


def _attention(q, k, v, batch, seq, heads, sum_lane):
    tq = MLA_Q_TILE
    hw = heads * LANES
    ow = heads * LANES // 2
    q3, k3, v3 = (a.reshape(batch, seq, hw) for a in (q, k, v))
    q_spec = pl.BlockSpec((None, tq, hw), lambda b, i: (b, i, 0))
    kv_spec = pl.BlockSpec((None, seq, hw), lambda b, i: (b, 0, 0))
    out = pl.pallas_call(
        functools.partial(_attn_kernel, heads=heads, sum_lane=sum_lane),
        grid=(batch, seq // tq),
        in_specs=[q_spec, kv_spec, kv_spec],
        out_specs=pl.BlockSpec((None, tq, ow), lambda b, i: (b, i, 0)),
        out_shape=jax.ShapeDtypeStruct((batch, seq, ow), BF16),
        compiler_params=_params("arbitrary", "arbitrary"),
        name="mla_attention",
    )(q3, k3, v3)
    return out.reshape(batch * seq, ow)


def _odd_in_kernel(x_ref, g_ref, w_ref, c_ref, slo_ref, shi_ref, q_ref, k_ref, v_ref):
    half = DIFF_HD // 2
    scale = DIFF_HD ** -0.5 * LOG2E
    for r in range(0, x_ref.shape[0], ROW_SUBTILE):
        rows = slice(r, r + ROW_SUBTILE)
        xn = _rms(x_ref[rows, :], g_ref[...], EPS).astype(BF16)
        qkv = _dot(xn, w_ref[...])
        c = c_ref[rows, :]
        s_lo = slo_ref[rows, :]
        s_hi = shi_ref[rows, :]
        for h in range(DIFF_HEADS):
            blk = slice(h * LANES, (h + 1) * LANES)
            kblk = slice(D_MODEL + h * LANES, D_MODEL + (h + 1) * LANES)
            q_ref[rows, blk] = (_rope(qkv[:, blk], c, s_lo, s_hi, half) * scale).astype(BF16)
            k_ref[rows, blk] = _rope(qkv[:, kblk], c, s_lo, s_hi, half).astype(BF16)
        v_ref[rows, :] = qkv[:, 2 * D_MODEL:].astype(BF16)


def _diff_rope_tables(seq):
    inv = ROPE_THETA ** (-jnp.arange(0, DIFF_HD, 2, dtype=F32) / DIFF_HD)
    ang = jnp.arange(seq, dtype=F32)[:, None] * inv[None, :]
    cos, sin = jnp.cos(ang), jnp.sin(ang)
    zero = jnp.zeros_like(sin)
    reps = LANES // DIFF_HD
    c = jnp.tile(jnp.concatenate([cos, cos], axis=1), (1, reps))
    s_lo = jnp.tile(jnp.concatenate([-sin, zero], axis=1), (1, reps))
    s_hi = jnp.tile(jnp.concatenate([zero, sin], axis=1), (1, reps))
    return c, s_lo, s_hi


def _odd_in(x2d, g, w_qkv, seq):
    rows = x2d.shape[0]
    tm = ROW_TILE
    c, s_lo, s_hi = _diff_rope_tables(seq)
    tiles_per_seq = seq // tm
    row_spec = lambda width: pl.BlockSpec((tm, width), lambda i: (i, 0))
    tab_spec = pl.BlockSpec((tm, LANES), lambda i: (i % tiles_per_seq, 0))
    w = w_qkv.astype(BF16)
    return pl.pallas_call(
        _odd_in_kernel,
        grid=(rows // tm,),
        in_specs=[row_spec(D_MODEL), _const_spec((1, D_MODEL)), _const_spec(w.shape),
                  tab_spec, tab_spec, tab_spec],
        out_specs=[row_spec(D_MODEL)] * 3,
        out_shape=[jax.ShapeDtypeStruct((rows, D_MODEL), BF16)] * 3,
        compiler_params=_params("arbitrary"),
        name="odd_in_proj",
    )(x2d, g[None, :], w, c, s_lo, s_hi)


def _diff_attn_kernel(lam_ref, q_ref, k_ref, v_ref, g_ref, o_ref, s_ref, e_ref, *, lam_init):
    lp = lam_ref[...]
    lam = (jnp.exp(jnp.sum(lp[0:1] * lp[1:2], axis=-1, keepdims=True))
           - jnp.exp(jnp.sum(lp[2:3] * lp[3:4], axis=-1, keepdims=True)) + lam_init)
    g = g_ref[...] * (1.0 - lam_init)
    tq = q_ref.shape[0]
    first_head = lax.broadcasted_iota(jnp.int32, (tq, LANES), 1) < DIFF_HD
    blocks = [slice(r, r + SOFTMAX_ROWS) for r in range(0, tq, SOFTMAX_ROWS)]

    def scores(h):
        blk = slice(h * LANES, (h + 1) * LANES)
        q = q_ref[:, blk]
        k = k_ref[:, blk]
        zero = jnp.zeros_like(q)
        s_ref[h % 2, 0] = _dot_nt(jnp.where(first_head, q, zero), k)
        s_ref[h % 2, 1] = _dot_nt(jnp.where(first_head, zero, q), k)

    def numerator(slot, j):
        m = jnp.max(s_ref[slot, j], axis=-1, keepdims=True)
        sums = []
        for rows in blocks:
            m_rows = m[rows]
            acc = None
            for c0 in range(0, s_ref.shape[-1], SOFTMAX_KEYS):
                keys = slice(c0, c0 + SOFTMAX_KEYS)
                e = jnp.exp2(s_ref[slot, j, rows, keys] - m_rows)
                for l0 in range(0, SOFTMAX_KEYS, LANES):
                    part = e[:, l0:l0 + LANES]
                    acc = part if acc is None else acc + part
                e_ref[j, rows, keys] = e.astype(BF16)
            sums.append(jnp.sum(acc, axis=-1, keepdims=True))
        return jnp.concatenate(sums, axis=0)

    scores(0)
    for h in range(DIFF_HEADS):
        if h + 1 < DIFF_HEADS:
            scores(h + 1)
        blk = slice(h * LANES, (h + 1) * LANES)
        l1 = numerator(h % 2, 0)
        l2 = numerator(h % 2, 1)
        c = (lam * l1 / l2).astype(BF16)
        for rows in blocks:
            e_ref[0, rows, :] = e_ref[0, rows, :] - c[rows] * e_ref[1, rows, :]
        o = _dot(e_ref[0], v_ref[:, blk]) / l1
        o_ref[:, blk] = (o * lax.rsqrt(jnp.mean(o * o, axis=-1, keepdims=True) + SUBLN_EPS) * g).astype(o_ref.dtype)


def _diff_attention(q, k, v, lam_params, subln, lam_init, batch, seq):
    tq = DIFF_Q_TILE
    q3, k3, v3 = (a.reshape(batch, seq, D_MODEL) for a in (q, k, v))
    q_spec = pl.BlockSpec((None, tq, D_MODEL), lambda b, i: (b, i, 0))
    kv_spec = pl.BlockSpec((None, seq, D_MODEL), lambda b, i: (b, 0, 0))
    lam_tile = jnp.pad(jnp.stack(lam_params).astype(F32), ((0, 4), (0, LANES - DIFF_HD)))
    out = pl.pallas_call(
        functools.partial(_diff_attn_kernel, lam_init=lam_init),
        grid=(batch, seq // tq),
        in_specs=[pl.BlockSpec((8, LANES), lambda b, i: (0, 0)), q_spec, kv_spec, kv_spec,
                  pl.BlockSpec((1, LANES), lambda b, i: (0, 0))],
        out_specs=q_spec,
        out_shape=jax.ShapeDtypeStruct((batch, seq, D_MODEL), BF16),
        scratch_shapes=[pltpu.VMEM((2, 2, tq, seq), F32), pltpu.VMEM((2, tq, seq), BF16)],
        compiler_params=_params("arbitrary", "arbitrary"),
        name="diff_attention",
    )(lam_tile, q3, k3, v3, subln[None, :])
    return out.reshape(batch * seq, D_MODEL)


def _mem_kv_kernel(mem_ref, g_ref, wkv_ref, kt_ref, v_ref):
    nb, n_mem, _ = mem_ref.shape
    mn = _rms(mem_ref[...].reshape(nb * n_mem, D_MODEL), g_ref[...], EPS).astype(BF16)
    for layer in range(wkv_ref.shape[0]):
        kv = _dot(mn, wkv_ref[layer].astype(BF16))
        for b in range(nb):
            rows = slice(b * n_mem, (b + 1) * n_mem)
            kt_ref[layer, b] = kv[rows, :D_MODEL].T.astype(BF16)
            v_ref[layer, b] = kv[rows, D_MODEL:].astype(BF16)


def _mem_kv(mem, mem_norm, xa_wkv):
    batch, n_mem, _ = mem.shape
    depth = xa_wkv.shape[0]
    w = xa_wkv
    nb = MEM_BATCH_TILE
    return pl.pallas_call(
        _mem_kv_kernel,
        grid=(batch // nb,),
        in_specs=[pl.BlockSpec((nb, n_mem, D_MODEL), lambda b: (b, 0, 0)), _const_spec((1, D_MODEL)),
                  _const_spec(w.shape)],
        out_specs=[pl.BlockSpec((depth, nb, D_MODEL, n_mem), lambda b: (0, b, 0, 0)),
                   pl.BlockSpec((depth, nb, n_mem, D_MODEL), lambda b: (0, b, 0, 0))],
        out_shape=[jax.ShapeDtypeStruct((depth, batch, D_MODEL, n_mem), BF16),
                   jax.ShapeDtypeStruct((depth, batch, n_mem, D_MODEL), BF16)],
        compiler_params=_params("arbitrary"),
        name="memory_kv",
    )(mem, mem_norm[None, :], w)


def _cross_mlp_kernel(*refs, n_mix, final_norm):
    x_ref, o_ref = refs[0], refs[-1]
    mix = refs[1:1 + 2 * n_mix]
    gc_ref, wq_ref, kt_ref, v_ref, wo_ref, gm_ref, wup_ref, wdn_ref, gf_ref = refs[1 + 2 * n_mix:-1]
    x = x_ref[...]
    for a_ref, w_ref in zip(mix[0::2], mix[1::2]):
        x = x + _dot(a_ref[...], w_ref[...])
    hn = _rms(x, gc_ref[...], EPS).astype(BF16)
    q = (_dot(hn, wq_ref[...]) * (X_HD ** -0.5 * LOG2E)).astype(BF16)
    heads = []
    for h in range(X_HEADS):
        blk = slice(h * X_HD, (h + 1) * X_HD)
        s = _dot(q[:, blk], kt_ref[blk, :])
        e = jnp.exp2(s - jnp.max(s, axis=-1, keepdims=True))
        l = jnp.sum(e, axis=-1, keepdims=True)
        heads.append((_dot(e.astype(BF16), v_ref[:, blk]) / l).astype(BF16))
    x = x + _dot(jnp.concatenate(heads, axis=1), wo_ref[...])
    hn = _rms(x, gm_ref[...], EPS).astype(BF16)
    u = jnp.maximum(_dot(hn, wup_ref[...]), 0.0)
    x = x + _dot((u * u).astype(BF16), wdn_ref[...])
    if final_norm:
        x = _rms(x, gf_ref[...], EPS)
    o_ref[...] = x


def _mix_cross_mlp(x2d, mix_pairs, g_cross, wq, kt, v, wo, g_mlp, w_up, w_down, g_final, seq, final_norm):
    rows = x2d.shape[0]
    tm = MLP_ROW_TILE
    tiles_per_seq = seq // tm
    n_mem = v.shape[1]
    row_spec = lambda width: pl.BlockSpec((tm, width), lambda i: (i, 0))
    vec_spec = _const_spec((1, D_MODEL))
    mix_specs, mix_args = [], []
    for a, w in mix_pairs:
        mix_specs += [row_spec(a.shape[1]), _const_spec(w.shape)]
        mix_args += [a, w]
    return pl.pallas_call(
        functools.partial(_cross_mlp_kernel, n_mix=len(mix_pairs), final_norm=final_norm),
        grid=(rows // tm,),
        in_specs=[row_spec(D_MODEL)] + mix_specs + [
            vec_spec, _const_spec((D_MODEL, D_MODEL)),
            pl.BlockSpec((None, D_MODEL, n_mem), lambda i: (i // tiles_per_seq, 0, 0)),
            pl.BlockSpec((None, n_mem, D_MODEL), lambda i: (i // tiles_per_seq, 0, 0)),
            _const_spec((D_MODEL, D_MODEL)), vec_spec, _const_spec((D_MODEL, D_FF)),
            _const_spec((D_FF, D_MODEL)), vec_spec],
        out_specs=row_spec(D_MODEL),
        out_shape=jax.ShapeDtypeStruct((rows, D_MODEL), F32),
        compiler_params=_params("arbitrary"),
        name="mix_cross_mlp",
    )(x2d, *mix_args, g_cross[None, :], wq.astype(BF16), kt, v, wo.astype(BF16), g_mlp[None, :],
      w_up.astype(BF16), w_down.astype(BF16), g_final[None, :])


def kernel(x, mem, ev_w_in, ev_conv_w, ev_conv_b, hy_w1, hy_b1, hy_w2, hy_b2, hy_w3, hy_freq, hy_skip,
           mla_q_norm, mla_w_uq, mla_kv_norm, mla_w_ukv, ev_w_out, od_w_qkv, dif_lq1, dif_lk1, dif_lq2,
           dif_lk2, dif_subln, od_w_out, norm_mix, norm_cross, norm_mlp, xa_wq, xa_wkv, xa_wo, mlp_up,
           mlp_down, mem_norm, final_norm):
    batch, seq, d = x.shape
    depth = norm_mix.shape[0]
    assert d == D_MODEL and all(seq % t == 0 for t in (ROW_TILE, MLP_ROW_TILE, MLA_Q_TILE, DIFF_Q_TILE))
    assert batch % MEM_BATCH_TILE == 0
    x2d = x.reshape(batch * seq, d)
    kt_all, v_all = _mem_kv(mem, mem_norm, xa_wkv)
    stacked = dict(od_w_qkv=od_w_qkv, od_w_out=od_w_out, xa_wq=xa_wq, xa_wo=xa_wo,
                   mlp_up=mlp_up, mlp_down=mlp_down)
    bf16_copies = {}

    def w16(name, index):
        return bf16_copies.get((name, index), stacked[name][index])

    for i in range(depth):
        j = i // 2
        if i % 2 == 0:
            spectra = _hyena_spectra(seq, hy_w1[j], hy_b1[j], hy_w2[j], hy_b2[j], hy_w3[j], hy_freq[j])
            hy, q, k, v = _even_in(x2d, norm_mix[i], ev_w_in[j], mla_q_norm[j], mla_w_uq[j],
                                   mla_kv_norm[j], mla_w_ukv[j], seq)
            wanted = []
            for layer in (i, i + 1):
                if layer < depth:
                    wanted += [(n, layer) for n in ("xa_wq", "xa_wo", "mlp_up", "mlp_down")]
            if i + 1 < depth:
                wanted += [("od_w_qkv", j), ("od_w_out", j)]
            n_out = ev_w_out.shape[1]
            cast = [(ev_w_out, j, 0, HY_CH), (ev_w_out, j, HY_CH, n_out - HY_CH)]
            cast += [(stacked[n], idx, 0, stacked[n].shape[1]) for n, idx in wanted]
            z, copies = _hyena(hy, ev_conv_w[j], ev_conv_b[j], spectra, hy_skip[j], batch, seq, cast=cast)
            bf16_copies.update(zip(wanted, copies[2:]))
            o = _attention(q, k, v, batch, seq, MLA_HEADS, sum_lane=MLA_V)
            mix = [(z.reshape(batch * seq, HY_CH), copies[0]), (o, copies[1])]
        else:
            lam_init = 0.8 - 0.6 * math.exp(-0.3 * i)
            q, k, v = _odd_in(x2d, norm_mix[i], w16("od_w_qkv", j), seq)
            o = _diff_attention(q, k, v, (dif_lq1[j], dif_lk1[j], dif_lq2[j], dif_lk2[j]), dif_subln[j],
                                lam_init, batch, seq)
            mix = [(o, w16("od_w_out", j).astype(BF16))]
        x2d = _mix_cross_mlp(x2d, mix, norm_cross[i], w16("xa_wq", i), kt_all[i], v_all[i], w16("xa_wo", i),
                             norm_mlp[i], w16("mlp_up", i), w16("mlp_down", i), final_norm, seq,
                             final_norm=(i == depth - 1))
    return x2d.reshape(batch, seq, d)
```
